```python
import math
import jax, jax.numpy as jnp
from jax import lax
import numpy as np

D_MODEL = 1024
BATCH = 2
SEQ = 16384
DEPTH = 1
DEC_BATCH = 8
DEC_SEQ = 16
PAST_LEN = 2048

CHUNK = 64
Q_BLOCK = 128
EPS = 1e-6
MLA_HEADS = 8
MLA_Q_RANK = 384
MLA_KV_RANK = 256
MLA_NOPE = 64
MLA_ROPE = 32
MLA_V = 64
MLA_SCALE = (MLA_NOPE + MLA_ROPE) ** -0.5
ROPE_THETA = 10000.0
DIFF_HEADS = 8
DIFF_HD = 64
DIFF_VD = 2 * DIFF_HD
DIFF_SCALE = DIFF_HD ** -0.5
REL_BUCKETS = 32
REL_MAX_DIST = 128
N_EXPERTS = 32
TOP_K = 4
D_EXPERT = 1024
SWIGLU_LIMIT = 7.0
SWIGLU_ALPHA = 1.702
MOE_BLOCK = 128
IN_SIZES = (MLA_Q_RANK, MLA_KV_RANK, MLA_ROPE, DIFF_HEADS * 2 * DIFF_HD, DIFF_HEADS * 2 * DIFF_HD, DIFF_HEADS * DIFF_VD, 2 * D_MODEL)
D_IN = sum(IN_SIZES)
NEG = -1e30

kernel_name = 'hybrid_mla_diffattn_moe_stream_step'


def _rmsnorm(x, g):
    xf = x.astype(jnp.float32)
    y = xf * lax.rsqrt(jnp.mean(xf * xf, axis=-1, keepdims=True) + EPS)
    return (y * g.astype(jnp.float32)).astype(x.dtype)


def _rope(x, pos):
    half = x.shape[-1] // 2
    inv = ROPE_THETA ** (-jnp.arange(half, dtype=jnp.float32) / half)
    ang = pos.astype(jnp.float32)[:, None] * inv[None, :]
    cos = jnp.cos(ang)[None, :, None, :]
    sin = jnp.sin(ang)[None, :, None, :]
    xf = x.astype(jnp.float32)
    x1, x2 = xf[..., :half], xf[..., half:]
    return jnp.concatenate([x1 * cos - x2 * sin, x1 * sin + x2 * cos], axis=-1).astype(x.dtype)


def _chunk_mask(qpos, kpos):
    return (kpos[None, :] // CHUNK) <= (qpos[:, None] // CHUNK)


def _rel_bucket(rel):
    nb = REL_BUCKETS // 2
    max_exact = nb // 2
    n = jnp.abs(rel)
    nf = jnp.maximum(n, 1).astype(jnp.float32)
    large = max_exact + (jnp.log(nf / max_exact) / math.log(REL_MAX_DIST / max_exact) * (nb - max_exact)).astype(jnp.int32)
    large = jnp.minimum(large, nb - 1)
    return jnp.where(rel > 0, nb, 0) + jnp.where(n < max_exact, n, large)


def _masked_softmax(s, mask):
    return jax.nn.softmax(jnp.where(mask, s, NEG), axis=-1)


def _sweep(fn, qs, qpos):
    sq = qpos.shape[0]
    if sq <= Q_BLOCK or sq % Q_BLOCK:
        return fn(qs, qpos)
    nb = sq // Q_BLOCK

    def split(a):
        return jnp.moveaxis(a.reshape(a.shape[0], nb, Q_BLOCK, *a.shape[2:]), 1, 0)

    out = lax.map(lambda a: fn(a[0], a[1]), (tuple(split(a) for a in qs), qpos.reshape(nb, Q_BLOCK)))
    out = jnp.moveaxis(out, 0, 1)
    return out.reshape(out.shape[0], sq, *out.shape[3:])


def _mla_attend(q_nope, q_rope, qpos, k_nope, k_rope, v, kpos):
    def blk(qs, qp):
        qn, qr = qs
        s = (jnp.einsum('bqhd,bkhd->bhqk', qn, k_nope).astype(jnp.float32)
             + jnp.einsum('bqhr,bkr->bhqk', qr, k_rope).astype(jnp.float32)) * MLA_SCALE
        p = _masked_softmax(s, _chunk_mask(qp, kpos)).astype(v.dtype)
        return jnp.einsum('bhqk,bkhd->bqhd', p, v)
    return _sweep(blk, (q_nope, q_rope), qpos)


def _diff_attend(q1, q2, qpos, k1, k2, v, kpos, rel_table, lam):
    def blk(qs, qp):
        a, b = qs
        bias = jnp.transpose(rel_table[_rel_bucket(kpos[None, :] - qp[:, None])], (2, 0, 1)).astype(jnp.float32)
        mask = _chunk_mask(qp, kpos)
        s1 = jnp.einsum('bqhd,bkhd->bhqk', a, k1).astype(jnp.float32) * DIFF_SCALE + bias
        s2 = jnp.einsum('bqhd,bkhd->bhqk', b, k2).astype(jnp.float32) * DIFF_SCALE + bias
        p = _masked_softmax(s1, mask) - lam * _masked_softmax(s2, mask)
        return jnp.einsum('bhqk,bkhd->bqhd', p.astype(v.dtype), v)
    return _sweep(blk, (q1, q2), qpos)


def _moe(t, w_router, b_router, w_e_in, b_e_in, w_e_down, b_e_down):
    def blk(tb):
        logits = (jnp.einsum('td,de->te', tb, w_router) + b_router).astype(jnp.float32)
        top_v, top_i = lax.top_k(logits, TOP_K)
        wts = jax.nn.softmax(top_v, axis=-1)
        gate = jnp.sum(jax.nn.one_hot(top_i, N_EXPERTS, dtype=jnp.float32) * wts[..., None], axis=1)
        hu = jnp.einsum('td,edf->tef', tb, w_e_in) + b_e_in
        g = jnp.minimum(hu[..., :D_EXPERT], SWIGLU_LIMIT)
        u = jnp.clip(hu[..., D_EXPERT:], -SWIGLU_LIMIT, SWIGLU_LIMIT)
        act = g * jax.nn.sigmoid(SWIGLU_ALPHA * g) * (u + 1)
        gate = gate.astype(act.dtype)
        return jnp.einsum('tef,te,efd->td', act, gate, w_e_down) + jnp.einsum('te,ed->td', gate, b_e_down)
    n = t.shape[0]
    if n > MOE_BLOCK and n % MOE_BLOCK == 0:
        out = lax.map(blk, t.reshape(n // MOE_BLOCK, MOE_BLOCK, t.shape[-1]))
        return out.reshape(n, t.shape[-1])
    return blk(t)


def _layer(x, c, pos, past, rel_table, lam_init, w_ada, b_ada, norm1_g, w_in, q_norm_g, w_uq,
           kv_norm_g, w_ukv, lam_q1, lam_k1, lam_q2, lam_k2, subln_g, w_o_mla, w_o_diff, w_out,
           norm2_g, w_router, b_router, w_e_in, b_e_in, w_e_down, b_e_down):
    b, s, d = x.shape
    mod = jnp.einsum('bc,ce->be', jax.nn.silu(c), w_ada) + b_ada
    sh_a, sc_a, g_a, sh_f, sc_f, g_f = [m[:, None, :] for m in jnp.split(mod, 6, axis=-1)]
    h = _rmsnorm(x, norm1_g) * (1 + sc_a) + sh_a
    z = jnp.einsum('bsd,de->bse', h, w_in)
    splits = [sum(IN_SIZES[:i + 1]) for i in range(len(IN_SIZES) - 1)]
    cq, ckv, kr, dq, dk, dv, gates = jnp.split(z, splits, axis=-1)
    q = jnp.einsum('bsr,re->bse', _rmsnorm(cq, q_norm_g), w_uq).reshape(b, s, MLA_HEADS, MLA_NOPE + MLA_ROPE)
    q_nope, q_rope = q[..., :MLA_NOPE], _rope(q[..., MLA_NOPE:], pos)
    lat = _rmsnorm(ckv, kv_norm_g)
    krope = _rope(kr[:, :, None, :], pos)[:, :, 0, :]
    dq = dq.reshape(b, s, DIFF_HEADS, 2 * DIFF_HD)
    dk = dk.reshape(b, s, DIFF_HEADS, 2 * DIFF_HD)
    dv = dv.reshape(b, s, DIFF_HEADS, DIFF_VD)
    if past is None:
        lat_all, krope_all, dk_all, dv_all, kpos = lat, krope, dk, dv, pos
    else:
        c_lat, c_kr, c_k, c_v = past
        lat_all = jnp.concatenate([c_lat.astype(lat.dtype), lat], axis=1)
        krope_all = jnp.concatenate([c_kr.astype(krope.dtype), krope], axis=1)
        dk_all = jnp.concatenate([c_k.astype(dk.dtype), dk], axis=1)
        dv_all = jnp.concatenate([c_v.astype(dv.dtype), dv], axis=1)
        kpos = jnp.concatenate([jnp.arange(c_lat.shape[1], dtype=jnp.int32), pos])
    kv = jnp.einsum('bkr,re->bke', lat_all, w_ukv).reshape(b, -1, MLA_HEADS, MLA_NOPE + MLA_V)
    o_mla = _mla_attend(q_nope, q_rope, pos, kv[..., :MLA_NOPE], krope_all, kv[..., MLA_NOPE:], kpos)
    o_mla = jnp.einsum('bse,ed->bsd', o_mla.reshape(b, s, MLA_HEADS * MLA_V), w_o_mla)
    lam = (jnp.exp(jnp.sum(lam_q1 * lam_k1).astype(jnp.float32))
           - jnp.exp(jnp.sum(lam_q2 * lam_k2).astype(jnp.float32)) + lam_init)
    o_diff = _diff_attend(dq[..., :DIFF_HD], dq[..., DIFF_HD:], pos, dk_all[..., :DIFF_HD], dk_all[..., DIFF_HD:],
                          dv_all, kpos, rel_table, lam)
    o_diff = _rmsnorm(o_diff, subln_g) * (1 - lam_init)
    o_diff = jnp.einsum('bse,ed->bsd', o_diff.reshape(b, s, DIFF_HEADS * DIFF_VD), w_o_diff)
    g_m, g_d = jnp.split(jax.nn.sigmoid(gates), 2, axis=-1)
    x = x + g_a * jnp.einsum('bse,ed->bsd', g_m * o_mla + g_d * o_diff, w_out)
    h2 = _rmsnorm(x, norm2_g) * (1 + sc_f) + sh_f
    f = _moe(h2.reshape(b * s, d), w_router, b_router, w_e_in, b_e_in, w_e_down, b_e_down).reshape(b, s, d)
    x = x + g_f * f
    return x, (lat, krope, dk, dv)


def setup_inputs(seed: int = 0) -> dict:
    key = jax.random.key(seed)
    ks = iter(jax.random.split(key, 64))

    def nrm(shape, scale=1.0):
        return jax.random.normal(next(ks), shape, jnp.float32) * scale

    L, D = DEPTH, D_MODEL
    return {
        'x_prompt': nrm((BATCH, SEQ, D)),
        'x_sample': nrm((DEC_BATCH, DEC_SEQ, D)),
        'cache_mla_latent': nrm((L, DEC_BATCH, PAST_LEN, MLA_KV_RANK)),
        'cache_mla_krope': nrm((L, DEC_BATCH, PAST_LEN, MLA_ROPE)),
        'cache_diff_k': nrm((L, DEC_BATCH, PAST_LEN, DIFF_HEADS, 2 * DIFF_HD)),
        'cache_diff_v': nrm((L, DEC_BATCH, PAST_LEN, DIFF_HEADS, DIFF_VD)),
        'c_prompt': nrm((BATCH, D)),
        'c_sample': nrm((DEC_BATCH, D)),
        'rel_table': nrm((REL_BUCKETS, DIFF_HEADS), 0.5),
        'final_norm_g': 1.0 + nrm((D,), 0.01),
        'w_ada': nrm((L, D, 6 * D), 0.5 * D ** -0.5),
        'b_ada': nrm((L, 6 * D), 0.01),
        'norm1_g': 1.0 + nrm((L, D), 0.01),
        'w_in': nrm((L, D, D_IN), D ** -0.5),
        'q_norm_g': 1.0 + nrm((L, MLA_Q_RANK), 0.01),
        'w_uq': nrm((L, MLA_Q_RANK, MLA_HEADS * (MLA_NOPE + MLA_ROPE)), MLA_Q_RANK ** -0.5),
        'kv_norm_g': 1.0 + nrm((L, MLA_KV_RANK), 0.01),
        'w_ukv': nrm((L, MLA_KV_RANK, MLA_HEADS * (MLA_NOPE + MLA_V)), MLA_KV_RANK ** -0.5),
        'lam_q1': nrm((L, DIFF_HD), 0.1),
        'lam_k1': nrm((L, DIFF_HD), 0.1),
        'lam_q2': nrm((L, DIFF_HD), 0.1),
        'lam_k2': nrm((L, DIFF_HD), 0.1),
        'subln_g': 1.0 + nrm((L, DIFF_VD), 0.01),
        'w_o_mla': nrm((L, MLA_HEADS * MLA_V, D), (MLA_HEADS * MLA_V) ** -0.5),
        'w_o_diff': nrm((L, DIFF_HEADS * DIFF_VD, D), (DIFF_HEADS * DIFF_VD) ** -0.5),
        'w_out': nrm((L, D, D), D ** -0.5),
        'norm2_g': 1.0 + nrm((L, D), 0.01),
        'w_router': nrm((L, D, N_EXPERTS), D ** -0.5),
        'b_router': nrm((L, N_EXPERTS), 0.01),
        'w_e_in': nrm((L, N_EXPERTS, D, 2 * D_EXPERT), D ** -0.5),
        'b_e_in': nrm((L, N_EXPERTS, 2 * D_EXPERT), 0.01),
        'w_e_down': nrm((L, N_EXPERTS, D_EXPERT, D), D_EXPERT ** -0.5),
        'b_e_down': nrm((L, N_EXPERTS, D), 0.01),
    }


def reference(x_prompt, x_sample, cache_mla_latent, cache_mla_krope, cache_diff_k, cache_diff_v,
              c_prompt, c_sample, rel_table, final_norm_g, w_ada, b_ada, norm1_g, w_in, q_norm_g, w_uq,
              kv_norm_g, w_ukv, lam_q1, lam_k1, lam_q2, lam_k2, subln_g, w_o_mla, w_o_diff, w_out,
              norm2_g, w_router, b_router, w_e_in, b_e_in, w_e_down, b_e_down):
    pos_p = jnp.arange(x_prompt.shape[1], dtype=jnp.int32)
    pos_s = cache_mla_latent.shape[2] + jnp.arange(x_sample.shape[1], dtype=jnp.int32)
    hp, hs = x_prompt, x_sample
    rows_p, rows_s = [], []
    for l in range(DEPTH):
        lam_init = 0.8 - 0.6 * math.exp(-0.3 * l)
        lw = (w_ada[l], b_ada[l], norm1_g[l], w_in[l], q_norm_g[l], w_uq[l], kv_norm_g[l], w_ukv[l],
              lam_q1[l], lam_k1[l], lam_q2[l], lam_k2[l], subln_g[l], w_o_mla[l], w_o_diff[l], w_out[l],
              norm2_g[l], w_router[l], b_router[l], w_e_in[l], b_e_in[l], w_e_down[l], b_e_down[l])
        hp, rp = _layer(hp, c_prompt, pos_p, None, rel_table, lam_init, *lw)
        past = (cache_mla_latent[l], cache_mla_krope[l], cache_diff_k[l], cache_diff_v[l])
        hs, rs = _layer(hs, c_sample, pos_s, past, rel_table, lam_init, *lw)
        rows_p.append(rp)
        rows_s.append(rs)
    y_prompt = _rmsnorm(hp, final_norm_g)
    y_sample = _rmsnorm(hs, final_norm_g)
    lat_p = jnp.stack([r[0] for r in rows_p])
    kr_p = jnp.stack([r[1] for r in rows_p])
    k_p = jnp.stack([r[2] for r in rows_p])
    v_p = jnp.stack([r[3] for r in rows_p])
    lat_s = jnp.stack([r[0] for r in rows_s])
    kr_s = jnp.stack([r[1] for r in rows_s])
    k_s = jnp.stack([r[2] for r in rows_s])
    v_s = jnp.stack([r[3] for r in rows_s])
    return (y_prompt, y_sample, lat_p, kr_p, k_p, v_p, lat_s, kr_s, k_s, v_s)
```

```python
import functools
import math

import jax
import jax.numpy as jnp
import numpy as np
from jax import lax
from jax.experimental import pallas as pl
from jax.experimental.pallas import tpu as pltpu

F32 = jnp.float32
BF16 = jnp.bfloat16

CHUNK = 64
EPS = 1e-6
HEADS = 8
MLA_Q_RANK = 384
MLA_KV_RANK = 256
MLA_NOPE = 64
MLA_ROPE = 32
MLA_V = 64
MLA_SCALE = (MLA_NOPE + MLA_ROPE) ** -0.5
ROPE_THETA = 10000.0
DIFF_HD = 64
DIFF_VD = 2 * DIFF_HD
DIFF_SCALE = DIFF_HD ** -0.5
REL_BUCKETS = 32
REL_MAX_DIST = 128
N_EXPERTS = 32
TOP_K = 4
SWIGLU_LIMIT = 7.0
SWIGLU_ALPHA = 1.702
NEG = -1e30

LANES = 128
SUBLANES_BF16 = 16
VMEM_LIMIT = 56 * 1024 * 1024

ROPE_LO = MLA_NOPE
ROPE_HI = MLA_NOPE + MLA_ROPE
CHUNK_SHIFT = CHUNK.bit_length() - 1
assert 1 << CHUNK_SHIFT == CHUNK

REL_HALF = REL_BUCKETS // 2
REL_EXACT = REL_HALF // 2
assert (REL_MAX_DIST // REL_EXACT) == 1 << (REL_HALF - REL_EXACT) // 2
REL_FAR = math.isqrt(REL_EXACT * REL_EXACT * 2 ** (REL_HALF - REL_EXACT - 1) - 1) + 1


def _params(sem, vmem=VMEM_LIMIT):
    return pltpu.CompilerParams(dimension_semantics=sem, vmem_limit_bytes=vmem)


def _const_spec(shape):
    zeros = (0,) * len(shape)
    return pl.BlockSpec(shape, lambda *_: zeros, pipeline_mode=pl.Buffered(1))


def _dot(a, b):
    return jnp.dot(a, b, preferred_element_type=F32)


def _dot_nt(a, b):
    return lax.dot_general(a, b, (((1,), (1,)), ((), ())), preferred_element_type=F32)


def _rms(x, g):
    return x * lax.rsqrt(jnp.mean(x * x, axis=-1, keepdims=True) + EPS) * g


def _split_bf16(x):
    hi = x.astype(BF16)
    lo = (x - hi.astype(F32)).astype(BF16)
    return hi, lo


def _rel_bucket(rel):
    n = jnp.abs(rel)
    n2 = n * n
    cnt = jnp.zeros_like(n)
    for j in range(1, REL_HALF - REL_EXACT):
        cnt = cnt + (n2 >= REL_EXACT * REL_EXACT * 2 ** j).astype(jnp.int32)
    large = REL_EXACT + cnt
    return jnp.where(rel > 0, REL_HALF, 0) + jnp.where(n < REL_EXACT, n, large)


def _bias_from_table(bucket, tab_ref, h):
    out = jnp.zeros(bucket.shape, F32)
    for b in range(REL_BUCKETS):
        out = jnp.where(bucket == b, tab_ref[b, h], out)
    return out


def _ada_kernel(c_ref, w_ref, b_ref, lamv_ref, mod_ref, lam_ref, *, lam_init):
    c = c_ref[...]
    s = (c * jax.nn.sigmoid(c)).astype(BF16)
    mod_ref[...] = _dot(s, w_ref[...]) + b_ref[...]

    @pl.when(pl.program_id(0) == 0)
    def _():
        v = lamv_ref[...]
        a = jnp.sum(v[0:1] * v[1:2], axis=1, keepdims=True)
        b = jnp.sum(v[2:3] * v[3:4], axis=1, keepdims=True)
        lam_ref[...] = jnp.broadcast_to(jnp.exp(a) - jnp.exp(b) + lam_init, lam_ref.shape)


def _ada(c_all, w_ada, b_ada, lam_vecs, lam_init):
    rows, d = c_all.shape
    n = w_ada.shape[1]
    tn = d
    return pl.pallas_call(
        functools.partial(_ada_kernel, lam_init=lam_init),
        grid=(n // tn,),
        in_specs=[
            pl.BlockSpec((rows, d), lambda j: (0, 0)),
            pl.BlockSpec((d, tn), lambda j: (0, j)),
            pl.BlockSpec((1, tn), lambda j: (0, j)),
            pl.BlockSpec(lam_vecs.shape, lambda j: (0, 0)),
        ],
        out_specs=[
            pl.BlockSpec((rows, tn), lambda j: (0, j)),
            pl.BlockSpec((8, LANES), lambda j: (0, 0)),
        ],
        out_shape=[
            jax.ShapeDtypeStruct((rows, n), F32),
            jax.ShapeDtypeStruct((8, LANES), F32),
        ],
        compiler_params=_params(("arbitrary",)),
        name="ada",
    )(c_all, w_ada, b_ada, lam_vecs)


def _rope_kernel(inv_ref, cos_ref, sin_ref):
    rows = cos_ref.shape[0]
    pos = pl.program_id(0) * rows + lax.broadcasted_iota(jnp.int32, (rows, LANES), 0)
    lane = lax.broadcasted_iota(jnp.int32, (rows, LANES), 1)
    ang = pos.astype(F32) * inv_ref[...]
    in_rope = (lane >= ROPE_LO) & (lane < ROPE_HI)
    first_half = lane < ROPE_LO + MLA_ROPE // 2
    cos_ref[...] = jnp.where(lane < ROPE_LO, 1.0, jnp.where(in_rope, jnp.cos(ang), 0.0))
    sn = jnp.sin(ang)
    sin_ref[...] = jnp.where(in_rope, jnp.where(first_half, -sn, sn), 0.0)


def _rope_tables(n_pos, rows):
    half = MLA_ROPE // 2
    inv = ROPE_THETA ** (-jnp.arange(half, dtype=F32) / half)
    inv_lanes = jnp.zeros((1, LANES), F32).at[0, ROPE_LO:ROPE_HI].set(jnp.concatenate([inv, inv]))
    return pl.pallas_call(
        _rope_kernel,
        grid=(n_pos // rows,),
        in_specs=[pl.BlockSpec((1, LANES), lambda i: (0, 0))],
        out_specs=[pl.BlockSpec((rows, LANES), lambda i: (i, 0))] * 2,
        out_shape=[jax.ShapeDtypeStruct((n_pos, LANES), F32)] * 2,
        compiler_params=_params(("parallel",)),
        name="rope_tables",
    )(inv_lanes)


def _proj_kernel(x_ref, mod_ref, cos_ref, sin_ref, n1_ref, qg_ref, kvg_ref,
                 w1_ref, wq_ref, wqs_ref, wk_ref, wv_ref, wdq_ref, wdk_ref, wdv_ref, wg_ref,
                 lat_ref, kr_ref, dkf_ref, dvf_ref,
                 qm_ref, km_ref, vm_ref, qd_ref, kd_ref, vd_ref, g_ref):
    m = mod_ref[0]
    h = (_rms(x_ref[0], n1_ref[...]) * (1.0 + m[1:2]) + m[0:1]).astype(BF16)
    cos = cos_ref[...]
    sin = sin_ref[...]

    z1 = _dot(h, w1_ref[...])
    r0 = MLA_Q_RANK
    r1 = r0 + MLA_KV_RANK
    cq, ckv = z1[:, :r0], z1[:, r0:r1]
    kr, krs = z1[:, r1:r1 + LANES], z1[:, r1 + LANES:]

    qc = _rms(cq, qg_ref[...]).astype(BF16)
    q = _dot(qc, wq_ref[...])
    qs = _dot(qc, wqs_ref[...])
    lat = _rms(ckv, kvg_ref[...])
    lat_ref[0] = lat
    latb = lat.astype(BF16)
    krope = kr * cos + krs * sin
    kr_ref[0] = krope[:, ROPE_LO:ROPE_HI]
    kn = _dot(latb, wk_ref[...])
    v = _dot(latb, wv_ref[...])
    dq = _dot(h, wdq_ref[...]) * DIFF_SCALE
    dk = _dot(h, wdk_ref[...])
    dv = _dot(h, wdv_ref[...])
    dkf_ref[0] = dk
    dvf_ref[0] = dv
    for hd in range(HEADS):
        sl = slice(hd * LANES, (hd + 1) * LANES)
        qm_ref[0, hd] = ((q[:, sl] * cos + qs[:, sl] * sin) * MLA_SCALE).astype(BF16)
        km_ref[0, hd] = (kn[:, sl] + krope).astype(BF16)
        vm_ref[0, hd] = v[:, sl].astype(BF16)
        qd_ref[0, hd] = dq[:, sl].astype(BF16)
        kd_ref[0, hd] = dk[:, sl].astype(BF16)
        vd_ref[0, hd] = dv[:, sl].astype(BF16)
    g_ref[0] = jax.nn.sigmoid(_dot(h, wg_ref[...])).astype(BF16)


def _proj(x, mod, cos_tab, sin_tab, pos0, tt, w):
    b, s, d = x.shape
    assert s % tt == 0 and pos0 % tt == 0
    off = pos0 // tt
    tok = lambda width: pl.BlockSpec((1, tt, width), lambda bi, i: (bi, i, 0))
    head = pl.BlockSpec((1, HEADS, tt, LANES), lambda bi, i: (bi, 0, i, 0))
    tab = pl.BlockSpec((tt, LANES), lambda bi, i: (off + i, 0))
    hm = jax.ShapeDtypeStruct((b, HEADS, s, LANES), BF16)
    weights = [w["n1"], w["qg"], w["kvg"], w["w1"], w["wq"], w["wqs"], w["wk"], w["wv"],
               w["wdq"], w["wdk"], w["wdv"], w["wg"]]
    return pl.pallas_call(
        _proj_kernel,
        grid=(b, s // tt),
        in_specs=[tok(d), pl.BlockSpec((1, 6, d), lambda bi, i: (bi, 0, 0)), tab, tab]
                 + [_const_spec(a.shape) for a in weights],
        out_specs=[tok(MLA_KV_RANK), tok(MLA_ROPE), tok(HEADS * LANES), tok(HEADS * LANES)]
                  + [head] * 6 + [tok(2 * d)],
        out_shape=[jax.ShapeDtypeStruct((b, s, MLA_KV_RANK), F32),
                   jax.ShapeDtypeStruct((b, s, MLA_ROPE), F32),
                   jax.ShapeDtypeStruct((b, s, HEADS * LANES), F32),
                   jax.ShapeDtypeStruct((b, s, HEADS * LANES), F32)]
                  + [hm] * 6 + [jax.ShapeDtypeStruct((b, s, 2 * d), BF16)],
        compiler_params=_params(("parallel", "parallel")),
        name="proj",
    )(x, mod, cos_tab, sin_tab, *weights)


def _online_softmax(s, v, m_ref, l_ref, acc_ref, h):
    m_prev = m_ref[h]
    m_new = jnp.maximum(m_prev, jnp.max(s, axis=1, keepdims=True))
    alpha = jnp.exp(m_prev - m_new)
    p = jnp.exp(s - m_new[:, :1])
    l_ref[h] = alpha * l_ref[h] + jnp.sum(p, axis=1, keepdims=True)
    acc_ref[h] = alpha * acc_ref[h] + _dot(p.astype(BF16), v)
    m_ref[h] = m_new


def _diff_queries(qd):
    lane = lax.broadcasted_iota(jnp.int32, (1, LANES), 1)
    first = (lane < DIFF_HD).astype(qd.dtype)
    return jnp.concatenate([qd * first, qd * (1 - first)], axis=0)


def _diff_output(o12, lam, sg, post_scale):
    rows = o12.shape[0] // 2
    o = o12[:rows] - lam * o12[rows:]
    return _rms(o, sg) * post_scale


def _attn_kernel(qi_ref, kj_ref, tab_ref, qm_ref, qd_ref, km_ref, vm_ref, kd_ref, vd_ref, lam_ref, sg_ref,
                 om_ref, od_ref,
                 bias_s, mask_s, m_m, l_m, acc_m, m_d, l_d, acc_d, *, post_scale):
    t = qm_ref.shape[2]
    p = pl.program_id(1)
    qi = qi_ref[p]
    kj = kj_ref[p]

    @pl.when((pl.program_id(0) == 0) & (p == 0))
    def _build_bias():
        row = lax.broadcasted_iota(jnp.int32, (t, t), 0)
        col = lax.broadcasted_iota(jnp.int32, (t, t), 1)
        masked = (col >> CHUNK_SHIFT) > (row >> CHUNK_SHIFT)
        mask_s[...] = jnp.where(masked, NEG, 0.0)
        for near in range(2):
            bucket = _rel_bucket(col - row - near * t)

            def per_head(h, carry):
                far = tab_ref[REL_HALF - 1, h]
                b = _bias_from_table(bucket, tab_ref, h) - far
                if near == 0:
                    b = jnp.where(masked, NEG, b)
                bias_s[near, h] = b
                return carry

            lax.fori_loop(0, HEADS, per_head, 0)

    @pl.when(kj == 0)
    def _init():
        m_m[...] = jnp.full(m_m.shape, NEG, F32)
        m_d[...] = jnp.full(m_d.shape, NEG, F32)
        l_m[...] = jnp.zeros(l_m.shape, F32)
        l_d[...] = jnp.zeros(l_d.shape, F32)
        acc_m[...] = jnp.zeros(acc_m.shape, F32)
        acc_d[...] = jnp.zeros(acc_d.shape, F32)

    def step(near):
        def per_head(h, carry):
            s = _dot_nt(qm_ref[0, h], km_ref[0, h])
            if near == 0:
                s = s + mask_s[...]
            _online_softmax(s, vm_ref[0, h], m_m, l_m, acc_m, h)
            s12 = _dot_nt(_diff_queries(qd_ref[0, h]), kd_ref[0, h])
            if near is not None:
                b = bias_s[near, h]
                s12 = s12 + jnp.concatenate([b, b], axis=0)
            _online_softmax(s12, vd_ref[0, h], m_d, l_d, acc_d, h)
            return carry

        lax.fori_loop(0, HEADS, per_head, 0)

    @pl.when(kj < qi - 1)
    def _far():
        step(None)

    @pl.when(kj == qi - 1)
    def _sub_diagonal():
        step(1)

    @pl.when(kj == qi)
    def _diagonal():
        step(0)
        lam = lam_ref[0:1, :]
        sg = sg_ref[...]

        def finish(h, carry):
            om_ref[0, h] = (acc_m[h] / l_m[h]).astype(om_ref.dtype)
            od_ref[0, h] = _diff_output(acc_d[h] / l_d[h], lam, sg, post_scale).astype(od_ref.dtype)
            return carry

        lax.fori_loop(0, HEADS, finish, 0)


def _attn_prompt(rel_table, qm, qd, km, vm, kd, vd, lamv, sg, t, post_scale):
    b, _, s, _ = qm.shape
    assert s % t == 0 and t % CHUNK == 0 and t % LANES == 0 and t + 1 >= REL_FAR
    nq = s // t
    qi = np.concatenate([np.full(i + 1, i, np.int32) for i in range(nq)])
    kj = np.concatenate([np.arange(i + 1, dtype=np.int32) for i in range(nq)])
    qspec = pl.BlockSpec((1, HEADS, t, LANES), lambda bi, p, qi_r, kj_r: (bi, 0, qi_r[p], 0))
    kspec = pl.BlockSpec((1, HEADS, t, LANES), lambda bi, p, qi_r, kj_r: (bi, 0, kj_r[p], 0))
    stat = lambda rows: pltpu.VMEM((HEADS, rows, LANES), F32)
    out = jax.ShapeDtypeStruct((b, HEADS, s, LANES), BF16)
    return pl.pallas_call(
        functools.partial(_attn_kernel, post_scale=post_scale),
        grid_spec=pltpu.PrefetchScalarGridSpec(
            num_scalar_prefetch=2,
            grid=(b, len(qi)),
            in_specs=[pl.BlockSpec(memory_space=pltpu.SMEM),
                      qspec, qspec, kspec, kspec, kspec, kspec,
                      pl.BlockSpec((8, LANES), lambda bi, p, qi_r, kj_r: (0, 0)),
                      pl.BlockSpec((1, LANES), lambda bi, p, qi_r, kj_r: (0, 0))],
            out_specs=[qspec, qspec],
            scratch_shapes=[pltpu.VMEM((2, HEADS, t, t), F32), pltpu.VMEM((t, t), F32),
                            stat(t), stat(t), stat(t), stat(2 * t), stat(2 * t), stat(2 * t)],
        ),
        out_shape=[out, out],
        compiler_params=_params(("arbitrary", "arbitrary")),
        name="attn_prompt",
    )(jnp.asarray(qi), jnp.asarray(kj), rel_table, qm, qd, km, vm, kd, vd, lamv, sg)


def _attn_sample_kernel(tab_ref, qm_ref, qd_ref, kmn_ref, vmn_ref, kdn_ref, vdn_ref,
                        latc_ref, krc_ref, kdc_ref, vdc_ref, wk_ref, wv_ref, lam_ref, sg_ref,
                        om_ref, od_ref, *, post_scale):
    h = pl.program_id(1)
    nq = qm_ref.shape[2]
    past = latc_ref.shape[1]

    latb = latc_ref[0].astype(BF16)
    r = lax.broadcasted_iota(jnp.int32, (MLA_ROPE, LANES), 0)
    c = lax.broadcasted_iota(jnp.int32, (MLA_ROPE, LANES), 1)
    place = (c == r + ROPE_LO).astype(BF16)
    kc = (_dot(latb, wk_ref[...]) + _dot(krc_ref[0].astype(BF16), place)).astype(BF16)
    vc = _dot(latb, wv_ref[...]).astype(BF16)

    def geometry(reps, n_keys, key0):
        row = lax.broadcasted_iota(jnp.int32, (reps * nq, n_keys), 0)
        qpos = past + jnp.where(row >= nq, row - nq, row)
        kpos = key0 + lax.broadcasted_iota(jnp.int32, (reps * nq, n_keys), 1)
        visible = (kpos >> CHUNK_SHIFT) <= (qpos >> CHUNK_SHIFT)
        return visible, kpos - qpos

    def softmax_pv(s_c, s_n, v_c, v_n):
        m = jnp.maximum(jnp.max(s_c, axis=1, keepdims=True), jnp.max(s_n, axis=1, keepdims=True))
        p_c = jnp.exp(s_c - m)
        p_n = jnp.exp(s_n - m)
        l = jnp.sum(p_c, axis=1, keepdims=True) + jnp.sum(p_n, axis=1, keepdims=True)
        return (_dot(p_c.astype(BF16), v_c) + _dot(p_n.astype(BF16), v_n)) / l

    q = qm_ref[0, 0]
    vis_c, _ = geometry(1, past, 0)
    vis_n, _ = geometry(1, nq, past)
    s_c = jnp.where(vis_c, _dot_nt(q, kc), NEG)
    s_n = jnp.where(vis_n, _dot_nt(q, kmn_ref[0, 0]), NEG)
    om_ref[0, 0] = softmax_pv(s_c, s_n, vc, vmn_ref[0, 0]).astype(om_ref.dtype)

    q12 = _diff_queries(qd_ref[0, 0])
    vis_c, rel_c = geometry(2, past, 0)
    vis_n, rel_n = geometry(2, nq, past)
    b_c = _bias_from_table(_rel_bucket(rel_c), tab_ref, h)
    b_n = _bias_from_table(_rel_bucket(rel_n), tab_ref, h)
    s_c = jnp.where(vis_c, _dot_nt(q12, kdc_ref[0].astype(BF16)) + b_c, NEG)
    s_n = jnp.where(vis_n, _dot_nt(q12, kdn_ref[0, 0]) + b_n, NEG)
    o12 = softmax_pv(s_c, s_n, vdc_ref[0].astype(BF16), vdn_ref[0, 0])
    od_ref[0, 0] = _diff_output(o12, lam_ref[0:1, :], sg_ref[...], post_scale).astype(od_ref.dtype)


def _attn_sample(rel_table, qm, qd, kmn, vmn, kdn, vdn, latc, krc, kdc, vdc, wk, wv, lamv, sg, post_scale):
    b, _, nq, _ = qm.shape
    past = latc.shape[1]
    new = pl.BlockSpec((1, 1, nq, LANES), lambda bi, h: (bi, h, 0, 0))
    cache_head = pl.BlockSpec((1, past, LANES), lambda bi, h: (bi, 0, h))
    out = jax.ShapeDtypeStruct((b, HEADS, nq, LANES), BF16)
    return pl.pallas_call(
        functools.partial(_attn_sample_kernel, post_scale=post_scale),
        grid=(b, HEADS),
        in_specs=[pl.BlockSpec(memory_space=pltpu.SMEM), new, new, new, new, new, new,
                  pl.BlockSpec((1, past, MLA_KV_RANK), lambda bi, h: (bi, 0, 0)),
                  pl.BlockSpec((1, past, MLA_ROPE), lambda bi, h: (bi, 0, 0)),
                  cache_head, cache_head,
                  pl.BlockSpec((MLA_KV_RANK, LANES), lambda bi, h: (0, h)),
                  pl.BlockSpec((MLA_KV_RANK, LANES), lambda bi, h: (0, h)),
                  pl.BlockSpec((8, LANES), lambda bi, h: (0, 0)),
                  pl.BlockSpec((1, LANES), lambda bi, h: (0, 0))],
        out_specs=[new, new],
        out_shape=[out, out],
        compiler_params=_params(("parallel", "parallel")),
        name="attn_sample",
    )(rel_table, qm, qd, kmn, vmn, kdn, vdn, latc, krc, kdc, vdc, wk, wv, lamv, sg)


def _post_kernel(x_ref, om_ref, od_ref, g_ref, mod_ref, wom_ref, wod_ref, wout_ref, n2_ref,
                 wrh_ref, wrl_ref, br_ref, x1_ref, h2_ref, gate_ref):
    d = x_ref.shape[2]
    om = jnp.concatenate([om_ref[0, hd] for hd in range(HEADS)], axis=1)
    od = jnp.concatenate([od_ref[0, hd] for hd in range(HEADS)], axis=1)
    g = g_ref[0].astype(F32)
    merged = g[:, :d] * _dot(om, wom_ref[...]) + g[:, d:] * _dot(od, wod_ref[...])
    m = mod_ref[0]
    x1 = x_ref[0] + m[2:3] * _dot(merged.astype(BF16), wout_ref[...])
    x1_ref[0] = x1
    h2 = _rms(x1, n2_ref[...]) * (1.0 + m[4:5]) + m[3:4]
    h2_ref[0] = h2.astype(BF16)

    hi, lo = _split_bf16(h2)
    wrh = wrh_ref[...]
    logits = _dot(hi, wrh) + (_dot(hi, wrl_ref[...]) + _dot(lo, wrh)) + br_ref[...]
    lane = lax.broadcasted_iota(jnp.int32, logits.shape, 1).astype(F32)
    work = logits
    tops, picks = [], []
    for _ in range(TOP_K):
        top = jnp.max(work, axis=1, keepdims=True)
        first = jnp.min(jnp.where(work == top, lane, float(N_EXPERTS)), axis=1, keepdims=True)
        pick = lane == first
        tops.append(top)
        picks.append(pick)
        work = jnp.where(pick, -jnp.inf, work)
    es = [jnp.exp(top - tops[0]) for top in tops]
    den = es[0]
    for e in es[1:]:
        den = den + e
    gate = jnp.zeros(logits.shape, F32)
    for e, pick in zip(es, picks):
        gate = jnp.where(pick, e / den, gate)
    gate_ref[0] = gate


def _post(x, om, od, g, mod, tt, w):
    b, s, d = x.shape
    tok = lambda width: pl.BlockSpec((1, tt, width), lambda bi, i: (bi, i, 0))
    head = pl.BlockSpec((1, HEADS, tt, LANES), lambda bi, i: (bi, 0, i, 0))
    weights = [w["wom"], w["wod"], w["wout"], w["n2"], w["wrh"], w["wrl"], w["br"]]
    return pl.pallas_call(
        _post_kernel,
        grid=(b, s // tt),
        in_specs=[tok(d), head, head, tok(2 * d), pl.BlockSpec((1, 6, d), lambda bi, i: (bi, 0, 0))]
                 + [_const_spec(a.shape) for a in weights],
        out_specs=[tok(d), tok(d), tok(N_EXPERTS)],
        out_shape=[jax.ShapeDtypeStruct((b, s, d), F32),
                   jax.ShapeDtypeStruct((b, s, d), BF16),
                   jax.ShapeDtypeStruct((b, s, N_EXPERTS), F32)],
        compiler_params=_params(("parallel", "parallel")),
        name="post",
    )(x, om, od, g, mod, *weights)


def _moe_kernel(h_ref, gate_ref, win_ref, bin_ref, wdn_ref, bdn_ref, f_ref, acc_ref, *, f_chunk):
    e = pl.program_id(1)
    gate = gate_ref[...]
    d_exp = wdn_ref.shape[1]

    @pl.when(e == 0)
    def _():
        hi, lo = _split_bf16(gate)
        bdn = bdn_ref[...]
        acc_ref[...] = _dot(hi, bdn) + _dot(lo, bdn)

    lane = lax.broadcasted_iota(jnp.int32, gate.shape, 1)
    ge = jnp.sum(jnp.where(lane == e, gate, 0.0), axis=1, keepdims=True)
    h = h_ref[...]
    for c0 in range(0, d_exp, f_chunk):
        c1 = c0 + f_chunk
        gl = _dot(h, win_ref[0, :, c0:c1]) + bin_ref[0, :, c0:c1]
        up = _dot(h, win_ref[0, :, d_exp + c0:d_exp + c1]) + bin_ref[0, :, d_exp + c0:d_exp + c1]
        gl = jnp.minimum(gl, SWIGLU_LIMIT)
        up = jnp.clip(up, -SWIGLU_LIMIT, SWIGLU_LIMIT)
        act = gl * jax.nn.sigmoid(SWIGLU_ALPHA * gl) * (up + 1.0) * ge
        acc_ref[...] += _dot(act.astype(BF16), wdn_ref[0, c0:c1, :])

    @pl.when(e == pl.num_programs(1) - 1)
    def _():
        f_ref[...] = acc_ref[...]


def _moe(h2, gate, w, tm):
    n, d = h2.shape
    n_exp, _, two_f = w["win"].shape
    d_exp = two_f // 2
    assert n % tm == 0
    return pl.pallas_call(
        functools.partial(_moe_kernel, f_chunk=min(512, d_exp)),
        grid=(n // tm, n_exp),
        in_specs=[pl.BlockSpec((tm, d), lambda i, e: (i, 0)),
                  pl.BlockSpec((tm, n_exp), lambda i, e: (i, 0)),
                  pl.BlockSpec((1, d, two_f), lambda i, e: (e, 0, 0)),
                  pl.BlockSpec((1, 1, two_f), lambda i, e: (e, 0, 0)),
                  pl.BlockSpec((1, d_exp, d), lambda i, e: (e, 0, 0)),
                  pl.BlockSpec((n_exp, d), lambda i, e: (0, 0))],
        out_specs=pl.BlockSpec((tm, d), lambda i, e: (i, 0)),
        out_shape=jax.ShapeDtypeStruct((n, d), F32),
        scratch_shapes=[pltpu.VMEM((tm, d), F32)],
        compiler_params=_params(("parallel", "arbitrary")),
        name="moe",
    )(h2, gate, w["win"], w["bin"], w["wdn"], w["bdn"])


def _final_kernel(x1_ref, f_ref, mod_ref, g_ref, y_ref, *, normalize):
    y = x1_ref[0] + mod_ref[0][5:6] * f_ref[0]
    y_ref[0] = _rms(y, g_ref[...]) if normalize else y


def _final(x1, f, mod, g, tt, normalize):
    b, s, d = x1.shape
    tok = pl.BlockSpec((1, tt, d), lambda bi, i: (bi, i, 0))
    return pl.pallas_call(
        functools.partial(_final_kernel, normalize=normalize),
        grid=(b, s // tt),
        in_specs=[tok, tok, pl.BlockSpec((1, 6, d), lambda bi, i: (bi, 0, 0)),
                  pl.BlockSpec((1, d), lambda bi, i: (0, 0))],
        out_specs=tok,
        out_shape=jax.ShapeDtypeStruct((b, s, d), F32),
        compiler_params=_params(("parallel", "parallel")),
        name="final",
    )(x1, f, mod, g)


def _layer_weights(l, w_in, q_norm_g, w_uq, kv_norm_g, w_ukv, norm1_g, subln_g, w_o_mla, w_o_diff, w_out,
                   norm2_g, w_router, b_router, w_e_in, b_e_in, w_e_down, b_e_down):
    d = w_in.shape[1]
    hw = HEADS * LANES
    sizes = (MLA_Q_RANK, MLA_KV_RANK, MLA_ROPE, hw, hw, hw, 2 * d)
    offs = np.concatenate([[0], np.cumsum(sizes)])
    pieces = [w_in[l][:, offs[i]:offs[i + 1]] for i in range(len(sizes))]
    w_cq, w_ckv, w_kr, w_dq, w_dk, w_dv, w_g = pieces
    half = MLA_ROPE // 2
    pad_lo = jnp.zeros((d, ROPE_LO), F32)
    pad_hi = jnp.zeros((d, LANES - ROPE_HI), F32)
    kr_lanes = jnp.concatenate([pad_lo, w_kr, pad_hi], axis=1)
    kr_swap = jnp.concatenate([pad_lo, w_kr[:, half:], w_kr[:, :half], pad_hi], axis=1)
    w1 = jnp.concatenate([w_cq, w_ckv, kr_lanes, kr_swap], axis=1)

    uq = w_uq[l].reshape(MLA_Q_RANK, HEADS, MLA_NOPE + MLA_ROPE)
    nope, rope = uq[..., :MLA_NOPE], uq[..., MLA_NOPE:]
    zq = lambda n: jnp.zeros((MLA_Q_RANK, HEADS, n), F32)
    wq = jnp.concatenate([nope, rope, zq(LANES - ROPE_HI)], axis=-1).reshape(MLA_Q_RANK, hw)
    wqs = jnp.concatenate([zq(ROPE_LO), rope[..., half:], rope[..., :half], zq(LANES - ROPE_HI)],
                          axis=-1).reshape(MLA_Q_RANK, hw)
    ukv = w_ukv[l].reshape(MLA_KV_RANK, HEADS, MLA_NOPE + MLA_V)
    zk = lambda n: jnp.zeros((MLA_KV_RANK, HEADS, n), F32)
    wk = jnp.concatenate([ukv[..., :MLA_NOPE], zk(LANES - MLA_NOPE)], axis=-1).reshape(MLA_KV_RANK, hw)
    wv = jnp.concatenate([ukv[..., MLA_NOPE:], zk(LANES - MLA_V)], axis=-1).reshape(MLA_KV_RANK, hw)
    wom = jnp.concatenate([w_o_mla[l].reshape(HEADS, MLA_V, d), jnp.zeros((HEADS, LANES - MLA_V, d), F32)],
                          axis=1).reshape(hw, d)
    wrh, wrl = _split_bf16(w_router[l])
    bf = lambda a: a.astype(BF16)
    row = lambda a: a.reshape(1, -1)
    return dict(
        n1=row(norm1_g[l]), qg=row(q_norm_g[l]), kvg=row(kv_norm_g[l]),
        w1=bf(w1), wq=bf(wq), wqs=bf(wqs), wk=bf(wk), wv=bf(wv),
        wdq=bf(w_dq), wdk=bf(w_dk), wdv=bf(w_dv), wg=bf(w_g),
        sg=row(subln_g[l]), wom=bf(wom), wod=bf(w_o_diff[l]), wout=bf(w_out[l]),
        n2=row(norm2_g[l]), wrh=wrh, wrl=wrl, br=row(b_router[l]),
        win=bf(w_e_in[l]), bin=b_e_in[l][:, None, :], wdn=bf(w_e_down[l]), bdn=bf(b_e_down[l]),
    )


def _tile(n, target):
    t = min(n, target)
    while n % t:
        t //= 2
    return t


def kernel(x_prompt, x_sample, cache_mla_latent, cache_mla_krope, cache_diff_k, cache_diff_v,
           c_prompt, c_sample, rel_table, final_norm_g, w_ada, b_ada, norm1_g, w_in, q_norm_g, w_uq,
           kv_norm_g, w_ukv, lam_q1, lam_k1, lam_q2, lam_k2, subln_g, w_o_mla, w_o_diff, w_out,
           norm2_g, w_router, b_router, w_e_in, b_e_in, w_e_down, b_e_down):
    depth = w_ada.shape[0]
    bp, sp, d = x_prompt.shape
    bs, ss, _ = x_sample.shape
    past = cache_mla_latent.shape[2]
    assert HEADS * DIFF_VD == HEADS * LANES and past % ss == 0 and ss % SUBLANES_BF16 == 0

    n_pos = max(sp, past + ss)
    rope_rows = _tile(n_pos, 2048)
    cos_tab, sin_tab = _rope_tables(-(-n_pos // rope_rows) * rope_rows, rope_rows)

    pad_rows = -(bp + bs) % SUBLANES_BF16
    c_all = jnp.concatenate([c_prompt, c_sample, jnp.zeros((pad_rows, d), F32)], axis=0)

    g_fin = final_norm_g.reshape(1, d)
    hp, hs = x_prompt, x_sample
    rows_p, rows_s = [], []
    for l in range(depth):
        lam_init = 0.8 - 0.6 * math.exp(-0.3 * l)
        w = _layer_weights(l, w_in, q_norm_g, w_uq, kv_norm_g, w_ukv, norm1_g, subln_g, w_o_mla, w_o_diff,
                           w_out, norm2_g, w_router, b_router, w_e_in, b_e_in, w_e_down, b_e_down)
        lam_vecs = jnp.stack([lam_q1[l], lam_k1[l], lam_q2[l], lam_k2[l]])
        mod, lamv = _ada(c_all, w_ada[l].astype(BF16), b_ada[l].reshape(1, -1), lam_vecs, lam_init)
        mod_p = mod[:bp].reshape(bp, 6, d)
        mod_s = mod[bp:bp + bs].reshape(bs, 6, d)
        post_scale = 1.0 - lam_init

        tt = _tile(sp, 256)
        lat, kr, dkf, dvf, qm, km, vm, qd, kd, vd, g = _proj(hp, mod_p, cos_tab, sin_tab, 0, tt, w)
        om, od = _attn_prompt(rel_table, qm, qd, km, vm, kd, vd, lamv, w["sg"], _tile(sp, 256), post_scale)
        x1, h2, gate = _post(hp, om, od, g, mod_p, tt, w)
        f = _moe(h2.reshape(bp * sp, d), gate.reshape(bp * sp, N_EXPERTS), w, _tile(bp * sp, 1024))
        hp = (x1, f.reshape(bp, sp, d), mod_p)
        rows_p.append((lat, kr, dkf.reshape(bp, sp, HEADS, 2 * DIFF_HD), dvf.reshape(bp, sp, HEADS, DIFF_VD)))

        lat, kr, dkf, dvf, qm, km, vm, qd, kd, vd, g = _proj(hs, mod_s, cos_tab, sin_tab, past, ss, w)
        om, od = _attn_sample(rel_table, qm, qd, km, vm, kd, vd,
                              cache_mla_latent[l], cache_mla_krope[l],
                              cache_diff_k[l].reshape(bs, past, HEADS * 2 * DIFF_HD),
                              cache_diff_v[l].reshape(bs, past, HEADS * DIFF_VD),
                              w["wk"], w["wv"], lamv, w["sg"], post_scale)
        x1, h2, gate = _post(hs, om, od, g, mod_s, ss, w)
        f = _moe(h2.reshape(bs * ss, d), gate.reshape(bs * ss, N_EXPERTS), w, bs * ss)
        hs = (x1, f.reshape(bs, ss, d), mod_s)
        rows_s.append((lat, kr, dkf.reshape(bs, ss, HEADS, 2 * DIFF_HD), dvf.reshape(bs, ss, HEADS, DIFF_VD)))

        if l + 1 < depth:
            hp = _final(*hp, g_fin, _tile(sp, 1024), normalize=False)
            hs = _final(*hs, g_fin, ss, normalize=False)

    y_prompt = _final(*hp, g_fin, _tile(sp, 1024), normalize=True)
    y_sample = _final(*hs, g_fin, ss, normalize=True)
    stack = lambda rows, i: jnp.stack([r[i] for r in rows])
    return (y_prompt, y_sample,
            stack(rows_p, 0), stack(rows_p, 1), stack(rows_p, 2), stack(rows_p, 3),
            stack(rows_s, 0), stack(rows_s, 1), stack(rows_s, 2), stack(rows_s, 3))
```

```python
import functools
import math

import jax
import jax.numpy as jnp
import numpy as np
from jax import lax
from jax.experimental import pallas as pl
from jax.experimental.pallas import tpu as pltpu

F32 = jnp.float32
BF16 = jnp.bfloat16

CHUNK = 64
EPS = 1e-6
HEADS = 8
MLA_Q_RANK = 384
MLA_KV_RANK = 256
MLA_NOPE = 64
MLA_ROPE = 32
MLA_V = 64
MLA_SCALE = (MLA_NOPE + MLA_ROPE) ** -0.5
ROPE_THETA = 10000.0
DIFF_HD = 64
DIFF_VD = 2 * DIFF_HD
DIFF_SCALE = DIFF_HD ** -0.5
REL_BUCKETS = 32
REL_MAX_DIST = 128
N_EXPERTS = 32
TOP_K = 4
SWIGLU_LIMIT = 7.0
SWIGLU_ALPHA = 1.702
NEG = -1e30

LANES = 128
SUBLANES_BF16 = 16
VMEM_LIMIT = 56 * 1024 * 1024

ROPE_LO = MLA_NOPE
ROPE_HI = MLA_NOPE + MLA_ROPE
CHUNK_SHIFT = CHUNK.bit_length() - 1
assert 1 << CHUNK_SHIFT == CHUNK

REL_HALF = REL_BUCKETS // 2
REL_EXACT = REL_HALF // 2
assert (REL_MAX_DIST // REL_EXACT) == 1 << (REL_HALF - REL_EXACT) // 2
REL_FAR = math.isqrt(REL_EXACT * REL_EXACT * 2 ** (REL_HALF - REL_EXACT - 1) - 1) + 1


def _params(sem, vmem=VMEM_LIMIT):
    return pltpu.CompilerParams(dimension_semantics=sem, vmem_limit_bytes=vmem)


def _const_spec(shape):
    zeros = (0,) * len(shape)
    return pl.BlockSpec(shape, lambda *_: zeros, pipeline_mode=pl.Buffered(1))


def _dot(a, b):
    return jnp.dot(a, b, preferred_element_type=F32)


def _dot_nt(a, b):
    return lax.dot_general(a, b, (((1,), (1,)), ((), ())), preferred_element_type=F32)


def _rms(x, g):
    return x * lax.rsqrt(jnp.mean(x * x, axis=-1, keepdims=True) + EPS) * g


def _split_bf16(x):
    hi = x.astype(BF16)
    lo = (x - hi.astype(F32)).astype(BF16)
    return hi, lo


def _rel_bucket(rel):
    n = jnp.abs(rel)
    n2 = n * n
    cnt = jnp.zeros_like(n)
    for j in range(1, REL_HALF - REL_EXACT):
        cnt = cnt + (n2 >= REL_EXACT * REL_EXACT * 2 ** j).astype(jnp.int32)
    large = REL_EXACT + cnt
    return jnp.where(rel > 0, REL_HALF, 0) + jnp.where(n < REL_EXACT, n, large)


def _bias_from_table(bucket, tab_ref, h):
    out = jnp.zeros(bucket.shape, F32)
    for b in range(REL_BUCKETS):
        out = jnp.where(bucket == b, tab_ref[b, h], out)
    return out


def _ada_kernel(c_ref, w_ref, b_ref, lamv_ref, mod_ref, lam_ref, *, lam_init):
    c = c_ref[...]
    s = (c * jax.nn.sigmoid(c)).astype(BF16)
    mod_ref[...] = _dot(s, w_ref[...]) + b_ref[...]

    @pl.when(pl.program_id(0) == 0)
    def _():
        v = lamv_ref[...]
        a = jnp.sum(v[0:1] * v[1:2], axis=1, keepdims=True)
        b = jnp.sum(v[2:3] * v[3:4], axis=1, keepdims=True)
        lam_ref[...] = jnp.broadcast_to(jnp.exp(a) - jnp.exp(b) + lam_init, lam_ref.shape)


def _ada(c_all, w_ada, b_ada, lam_vecs, lam_init):
    rows, d = c_all.shape
    n = w_ada.shape[1]
    tn = d
    return pl.pallas_call(
        functools.partial(_ada_kernel, lam_init=lam_init),
        grid=(n // tn,),
        in_specs=[
            pl.BlockSpec((rows, d), lambda j: (0, 0)),
            pl.BlockSpec((d, tn), lambda j: (0, j)),
            pl.BlockSpec((1, tn), lambda j: (0, j)),
            pl.BlockSpec(lam_vecs.shape, lambda j: (0, 0)),
        ],
        out_specs=[
            pl.BlockSpec((rows, tn), lambda j: (0, j)),
            pl.BlockSpec((8, LANES), lambda j: (0, 0)),
        ],
        out_shape=[
            jax.ShapeDtypeStruct((rows, n), F32),
            jax.ShapeDtypeStruct((8, LANES), F32),
        ],
        compiler_params=_params(("arbitrary",)),
        name="ada",
    )(c_all, w_ada, b_ada, lam_vecs)


def _rope_kernel(inv_ref, cos_ref, sin_ref):
    rows = cos_ref.shape[0]
    pos = pl.program_id(0) * rows + lax.broadcasted_iota(jnp.int32, (rows, LANES), 0)
    lane = lax.broadcasted_iota(jnp.int32, (rows, LANES), 1)
    ang = pos.astype(F32) * inv_ref[...]
    in_rope = (lane >= ROPE_LO) & (lane < ROPE_HI)
    first_half = lane < ROPE_LO + MLA_ROPE // 2
    cos_ref[...] = jnp.where(lane < ROPE_LO, 1.0, jnp.where(in_rope, jnp.cos(ang), 0.0))
    sn = jnp.sin(ang)
    sin_ref[...] = jnp.where(in_rope, jnp.where(first_half, -sn, sn), 0.0)


def _rope_tables(n_pos, rows):
    half = MLA_ROPE // 2
    inv = ROPE_THETA ** (-jnp.arange(half, dtype=F32) / half)
    inv_lanes = jnp.zeros((1, LANES), F32).at[0, ROPE_LO:ROPE_HI].set(jnp.concatenate([inv, inv]))
    return pl.pallas_call(
        _rope_kernel,
        grid=(n_pos // rows,),
        in_specs=[pl.BlockSpec((1, LANES), lambda i: (0, 0))],
        out_specs=[pl.BlockSpec((rows, LANES), lambda i: (i, 0))] * 2,
        out_shape=[jax.ShapeDtypeStruct((n_pos, LANES), F32)] * 2,
        compiler_params=_params(("parallel",)),
        name="rope_tables",
    )(inv_lanes)


def _proj_kernel(x_ref, mod_ref, cos_ref, sin_ref, n1_ref, qg_ref, kvg_ref,
                 w1_ref, wq_ref, wqs_ref, wk_ref, wv_ref, wdq_ref, wdk_ref, wdv_ref, wg_ref,
                 lat_ref, kr_ref, dkf_ref, dvf_ref,
                 qm_ref, km_ref, vm_ref, qd_ref, kd_ref, vd_ref, g_ref):
    m = mod_ref[0]
    h = (_rms(x_ref[0], n1_ref[...]) * (1.0 + m[1:2]) + m[0:1]).astype(BF16)
    cos = cos_ref[...]
    sin = sin_ref[...]

    z1 = _dot(h, w1_ref[...])
    r0 = MLA_Q_RANK
    r1 = r0 + MLA_KV_RANK
    cq, ckv = z1[:, :r0], z1[:, r0:r1]
    kr, krs = z1[:, r1:r1 + LANES], z1[:, r1 + LANES:]

    qc = _rms(cq, qg_ref[...]).astype(BF16)
    q = _dot(qc, wq_ref[...])
    qs = _dot(qc, wqs_ref[...])
    lat = _rms(ckv, kvg_ref[...])
    lat_ref[0] = lat
    latb = lat.astype(BF16)
    krope = kr * cos + krs * sin
    kr_ref[0] = krope[:, ROPE_LO:ROPE_HI]
    kn = _dot(latb, wk_ref[...])
    v = _dot(latb, wv_ref[...])
    dq = _dot(h, wdq_ref[...]) * DIFF_SCALE
    dk = _dot(h, wdk_ref[...])
    dv = _dot(h, wdv_ref[...])
    dkf_ref[0] = dk
    dvf_ref[0] = dv
    for hd in range(HEADS):
        sl = slice(hd * LANES, (hd + 1) * LANES)
        qm_ref[0, hd] = ((q[:, sl] * cos + qs[:, sl] * sin) * MLA_SCALE).astype(BF16)
        km_ref[0, hd] = (kn[:, sl] + krope).astype(BF16)
        vm_ref[0, hd] = v[:, sl].astype(BF16)
        qd_ref[0, hd] = dq[:, sl].astype(BF16)
        kd_ref[0, hd] = dk[:, sl].astype(BF16)
        vd_ref[0, hd] = dv[:, sl].astype(BF16)
    g_ref[0] = jax.nn.sigmoid(_dot(h, wg_ref[...])).astype(BF16)


def _proj(x, mod, cos_tab, sin_tab, pos0, tt, w):
    b, s, d = x.shape
    assert s % tt == 0 and pos0 % tt == 0
    off = pos0 // tt
    tok = lambda width: pl.BlockSpec((1, tt, width), lambda bi, i: (bi, i, 0))
    head = pl.BlockSpec((1, HEADS, tt, LANES), lambda bi, i: (bi, 0, i, 0))
    tab = pl.BlockSpec((tt, LANES), lambda bi, i: (off + i, 0))
    hm = jax.ShapeDtypeStruct((b, HEADS, s, LANES), BF16)
    weights = [w["n1"], w["qg"], w["kvg"], w["w1"], w["wq"], w["wqs"], w["wk"], w["wv"],
               w["wdq"], w["wdk"], w["wdv"], w["wg"]]
    return pl.pallas_call(
        _proj_kernel,
        grid=(b, s // tt),
        in_specs=[tok(d), pl.BlockSpec((1, 6, d), lambda bi, i: (bi, 0, 0)), tab, tab]
                 + [_const_spec(a.shape) for a in weights],
        out_specs=[tok(MLA_KV_RANK), tok(MLA_ROPE), tok(HEADS * LANES), tok(HEADS * LANES)]
                  + [head] * 6 + [tok(2 * d)],
        out_shape=[jax.ShapeDtypeStruct((b, s, MLA_KV_RANK), F32),
                   jax.ShapeDtypeStruct((b, s, MLA_ROPE), F32),
                   jax.ShapeDtypeStruct((b, s, HEADS * LANES), F32),
                   jax.ShapeDtypeStruct((b, s, HEADS * LANES), F32)]
                  + [hm] * 6 + [jax.ShapeDtypeStruct((b, s, 2 * d), BF16)],
        compiler_params=_params(("parallel", "parallel")),
        name="proj",
    )(x, mod, cos_tab, sin_tab, *weights)


def _online_softmax(s, v, m_ref, l_ref, acc_ref, h):
    m_prev = m_ref[h]
    m_new = jnp.maximum(m_prev, jnp.max(s, axis=1, keepdims=True))
    alpha = jnp.exp(m_prev - m_new)
    p = jnp.exp(s - jnp.tile(m_new, (1, s.shape[1] // LANES)))
    l_ref[h] = alpha * l_ref[h] + jnp.sum(p, axis=1, keepdims=True)
    acc_ref[h] = alpha * acc_ref[h] + _dot(p.astype(BF16), v)
    m_ref[h] = m_new


def _diff_queries(qd):
    lane = lax.broadcasted_iota(jnp.int32, (1, LANES), 1)
    first = (lane < DIFF_HD).astype(qd.dtype)
    return jnp.concatenate([qd * first, qd * (1 - first)], axis=0)


def _diff_output(o12, lam, sg, post_scale):
    rows = o12.shape[0] // 2
    o = o12[:rows] - lam * o12[rows:]
    return _rms(o, sg) * post_scale


def _attn_kernel(qi_ref, kj_ref, tab_ref, qm_ref, qd_ref, km_ref, vm_ref, kd_ref, vd_ref, lam_ref, sg_ref,
                 om_ref, od_ref,
                 bias_s, mask_s, m_m, l_m, acc_m, m_d, l_d, acc_d, *, post_scale):
    t = qm_ref.shape[2]
    p = pl.program_id(1)
    qi = qi_ref[p]
    kj = kj_ref[p]

    @pl.when((pl.program_id(0) == 0) & (p == 0))
    def _build_bias():
        row = lax.broadcasted_iota(jnp.int32, (t, t), 0)
        col = lax.broadcasted_iota(jnp.int32, (t, t), 1)
        masked = (col >> CHUNK_SHIFT) > (row >> CHUNK_SHIFT)
        mask_s[...] = jnp.where(masked, NEG, 0.0)
        for near in range(2):
            bucket = _rel_bucket(col - row - near * t)

            def per_head(h, carry):
                far = tab_ref[REL_HALF - 1, h]
                b = _bias_from_table(bucket, tab_ref, h) - far
                if near == 0:
                    b = jnp.where(masked, NEG, b)
                bias_s[near, h] = b
                return carry

            lax.fori_loop(0, HEADS, per_head, 0)

    @pl.when(kj == 0)
    def _init():
        m_m[...] = jnp.full(m_m.shape, NEG, F32)
        m_d[...] = jnp.full(m_d.shape, NEG, F32)
        l_m[...] = jnp.zeros(l_m.shape, F32)
        l_d[...] = jnp.zeros(l_d.shape, F32)
        acc_m[...] = jnp.zeros(acc_m.shape, F32)
        acc_d[...] = jnp.zeros(acc_d.shape, F32)

    def step(near):
        def per_head(h, carry):
            s = _dot_nt(qm_ref[0, h], km_ref[0, h])
            if near == 0:
                s = s + mask_s[...]
            _online_softmax(s, vm_ref[0, h], m_m, l_m, acc_m, h)
            s12 = _dot_nt(_diff_queries(qd_ref[0, h]), kd_ref[0, h])
            if near is not None:
                b = bias_s[near, h]
                s12 = s12 + jnp.concatenate([b, b], axis=0)
            _online_softmax(s12, vd_ref[0, h], m_d, l_d, acc_d, h)
            return carry

        for h in range(HEADS):
            per_head(h, 0)

    @pl.when(kj < qi - 1)
    def _far():
        step(None)

    @pl.when(kj == qi - 1)
    def _sub_diagonal():
        step(1)

    @pl.when(kj == qi)
    def _diagonal():
        step(0)
        lam = lam_ref[0:1, :]
        sg = sg_ref[...]

        def finish(h, carry):
            om_ref[0, h] = (acc_m[h] / l_m[h]).astype(om_ref.dtype)
            od_ref[0, h] = _diff_output(acc_d[h] / l_d[h], lam, sg, post_scale).astype(od_ref.dtype)
            return carry

        lax.fori_loop(0, HEADS, finish, 0)


def _attn_prompt(rel_table, qm, qd, km, vm, kd, vd, lamv, sg, t, post_scale):
    b, _, s, _ = qm.shape
    assert s % t == 0 and t % CHUNK == 0 and t % LANES == 0 and t + 1 >= REL_FAR
    nq = s // t
    qi = np.concatenate([np.full(i + 1, i, np.int32) for i in range(nq)])
    kj = np.concatenate([np.arange(i + 1, dtype=np.int32) for i in range(nq)])
    qspec = pl.BlockSpec((1, HEADS, t, LANES), lambda bi, p, qi_r, kj_r: (bi, 0, qi_r[p], 0))
    kspec = pl.BlockSpec((1, HEADS, t, LANES), lambda bi, p, qi_r, kj_r: (bi, 0, kj_r[p], 0))
    stat = lambda rows: pltpu.VMEM((HEADS, rows, LANES), F32)
    out = jax.ShapeDtypeStruct((b, HEADS, s, LANES), BF16)
    return pl.pallas_call(
        functools.partial(_attn_kernel, post_scale=post_scale),
        grid_spec=pltpu.PrefetchScalarGridSpec(
            num_scalar_prefetch=2,
            grid=(b, len(qi)),
            in_specs=[pl.BlockSpec(memory_space=pltpu.SMEM),
                      qspec, qspec, kspec, kspec, kspec, kspec,
                      pl.BlockSpec((8, LANES), lambda bi, p, qi_r, kj_r: (0, 0)),
                      pl.BlockSpec((1, LANES), lambda bi, p, qi_r, kj_r: (0, 0))],
            out_specs=[qspec, qspec],
            scratch_shapes=[pltpu.VMEM((2, HEADS, t, t), F32), pltpu.VMEM((t, t), F32),
                            stat(t), stat(t), stat(t), stat(2 * t), stat(2 * t), stat(2 * t)],
        ),
        out_shape=[out, out],
        compiler_params=_params(("arbitrary", "arbitrary")),
        name="attn_prompt",
    )(jnp.asarray(qi), jnp.asarray(kj), rel_table, qm, qd, km, vm, kd, vd, lamv, sg)


def _attn_sample_kernel(tab_ref, qm_ref, qd_ref, kmn_ref, vmn_ref, kdn_ref, vdn_ref,
                        latc_ref, krc_ref, kdc_ref, vdc_ref, wk_ref, wv_ref, lam_ref, sg_ref,
                        om_ref, od_ref, *, post_scale):
    h = pl.program_id(1)
    nq = qm_ref.shape[2]
    past = latc_ref.shape[1]

    latb = latc_ref[0].astype(BF16)
    r = lax.broadcasted_iota(jnp.int32, (MLA_ROPE, LANES), 0)
    c = lax.broadcasted_iota(jnp.int32, (MLA_ROPE, LANES), 1)
    place = (c == r + ROPE_LO).astype(BF16)
    kc = (_dot(latb, wk_ref[...]) + _dot(krc_ref[0].astype(BF16), place)).astype(BF16)
    vc = _dot(latb, wv_ref[...]).astype(BF16)

    def geometry(reps, n_keys, key0):
        row = lax.broadcasted_iota(jnp.int32, (reps * nq, n_keys), 0)
        qpos = past + jnp.where(row >= nq, row - nq, row)
        kpos = key0 + lax.broadcasted_iota(jnp.int32, (reps * nq, n_keys), 1)
        visible = (kpos >> CHUNK_SHIFT) <= (qpos >> CHUNK_SHIFT)
        return visible, kpos - qpos

    def softmax_pv(s_c, s_n, v_c, v_n):
        m = jnp.maximum(jnp.max(s_c, axis=1, keepdims=True), jnp.max(s_n, axis=1, keepdims=True))
        p_c = jnp.exp(s_c - m)
        p_n = jnp.exp(s_n - m)
        l = jnp.sum(p_c, axis=1, keepdims=True) + jnp.sum(p_n, axis=1, keepdims=True)
        return (_dot(p_c.astype(BF16), v_c) + _dot(p_n.astype(BF16), v_n)) / l

    q = qm_ref[0, 0]
    vis_c, _ = geometry(1, past, 0)
    vis_n, _ = geometry(1, nq, past)
    s_c = jnp.where(vis_c, _dot_nt(q, kc), NEG)
    s_n = jnp.where(vis_n, _dot_nt(q, kmn_ref[0, 0]), NEG)
    om_ref[0, 0] = softmax_pv(s_c, s_n, vc, vmn_ref[0, 0]).astype(om_ref.dtype)

    q12 = _diff_queries(qd_ref[0, 0])
    vis_c, rel_c = geometry(2, past, 0)
    vis_n, rel_n = geometry(2, nq, past)
    b_c = _bias_from_table(_rel_bucket(rel_c), tab_ref, h)
    b_n = _bias_from_table(_rel_bucket(rel_n), tab_ref, h)
    s_c = jnp.where(vis_c, _dot_nt(q12, kdc_ref[0].astype(BF16)) + b_c, NEG)
    s_n = jnp.where(vis_n, _dot_nt(q12, kdn_ref[0, 0]) + b_n, NEG)
    o12 = softmax_pv(s_c, s_n, vdc_ref[0].astype(BF16), vdn_ref[0, 0])
    od_ref[0, 0] = _diff_output(o12, lam_ref[0:1, :], sg_ref[...], post_scale).astype(od_ref.dtype)


def _attn_sample(rel_table, qm, qd, kmn, vmn, kdn, vdn, latc, krc, kdc, vdc, wk, wv, lamv, sg, post_scale):
    b, _, nq, _ = qm.shape
    past = latc.shape[1]
    new = pl.BlockSpec((1, 1, nq, LANES), lambda bi, h: (bi, h, 0, 0))
    cache_head = pl.BlockSpec((1, past, LANES), lambda bi, h: (bi, 0, h))
    out = jax.ShapeDtypeStruct((b, HEADS, nq, LANES), BF16)
    return pl.pallas_call(
        functools.partial(_attn_sample_kernel, post_scale=post_scale),
        grid=(b, HEADS),
        in_specs=[pl.BlockSpec(memory_space=pltpu.SMEM), new, new, new, new, new, new,
                  pl.BlockSpec((1, past, MLA_KV_RANK), lambda bi, h: (bi, 0, 0)),
                  pl.BlockSpec((1, past, MLA_ROPE), lambda bi, h: (bi, 0, 0)),
                  cache_head, cache_head,
                  pl.BlockSpec((MLA_KV_RANK, LANES), lambda bi, h: (0, h)),
                  pl.BlockSpec((MLA_KV_RANK, LANES), lambda bi, h: (0, h)),
                  pl.BlockSpec((8, LANES), lambda bi, h: (0, 0)),
                  pl.BlockSpec((1, LANES), lambda bi, h: (0, 0))],
        out_specs=[new, new],
        out_shape=[out, out],
        compiler_params=_params(("parallel", "parallel")),
        name="attn_sample",
    )(rel_table, qm, qd, kmn, vmn, kdn, vdn, latc, krc, kdc, vdc, wk, wv, lamv, sg)


def _post_kernel(x_ref, om_ref, od_ref, g_ref, mod_ref, wom_ref, wod_ref, wout_ref, n2_ref,
                 wrh_ref, wrl_ref, br_ref, x1_ref, h2_ref, gate_ref):
    d = x_ref.shape[2]
    om = jnp.concatenate([om_ref[0, hd] for hd in range(HEADS)], axis=1)
    od = jnp.concatenate([od_ref[0, hd] for hd in range(HEADS)], axis=1)
    g = g_ref[0].astype(F32)
    merged = g[:, :d] * _dot(om, wom_ref[...]) + g[:, d:] * _dot(od, wod_ref[...])
    m = mod_ref[0]
    x1 = x_ref[0] + m[2:3] * _dot(merged.astype(BF16), wout_ref[...])
    x1_ref[0] = x1
    h2 = _rms(x1, n2_ref[...]) * (1.0 + m[4:5]) + m[3:4]
    h2_ref[0] = h2.astype(BF16)

    hi, lo = _split_bf16(h2)
    wrh = wrh_ref[...]
    logits = _dot(hi, wrh) + (_dot(hi, wrl_ref[...]) + _dot(lo, wrh)) + br_ref[...]
    lane = lax.broadcasted_iota(jnp.int32, logits.shape, 1).astype(F32)
    work = logits
    tops, picks = [], []
    for _ in range(TOP_K):
        top = jnp.max(work, axis=1, keepdims=True)
        first = jnp.min(jnp.where(work == top, lane, float(N_EXPERTS)), axis=1, keepdims=True)
        pick = lane == first
        tops.append(top)
        picks.append(pick)
        work = jnp.where(pick, -jnp.inf, work)
    es = [jnp.exp(top - tops[0]) for top in tops]
    den = es[0]
    for e in es[1:]:
        den = den + e
    gate = jnp.zeros(logits.shape, F32)
    for e, pick in zip(es, picks):
        gate = jnp.where(pick, e / den, gate)
    gate_ref[0] = gate


def _post(x, om, od, g, mod, tt, w):
    b, s, d = x.shape
    tok = lambda width: pl.BlockSpec((1, tt, width), lambda bi, i: (bi, i, 0))
    head = pl.BlockSpec((1, HEADS, tt, LANES), lambda bi, i: (bi, 0, i, 0))
    weights = [w["wom"], w["wod"], w["wout"], w["n2"], w["wrh"], w["wrl"], w["br"]]
    return pl.pallas_call(
        _post_kernel,
        grid=(b, s // tt),
        in_specs=[tok(d), head, head, tok(2 * d), pl.BlockSpec((1, 6, d), lambda bi, i: (bi, 0, 0))]
                 + [_const_spec(a.shape) for a in weights],
        out_specs=[tok(d), tok(d), tok(N_EXPERTS)],
        out_shape=[jax.ShapeDtypeStruct((b, s, d), F32),
                   jax.ShapeDtypeStruct((b, s, d), BF16),
                   jax.ShapeDtypeStruct((b, s, N_EXPERTS), F32)],
        compiler_params=_params(("parallel", "parallel")),
        name="post",
    )(x, om, od, g, mod, *weights)


def _moe_kernel(h_ref, gate_ref, win_ref, bin_ref, wdn_ref, bdn_ref, f_ref, acc_ref, *, f_chunk):
    e = pl.program_id(1)
    gate = gate_ref[...]
    d_exp = wdn_ref.shape[1]

    @pl.when(e == 0)
    def _():
        hi, lo = _split_bf16(gate)
        bdn = bdn_ref[...]
        acc_ref[...] = _dot(hi, bdn) + _dot(lo, bdn)

    lane = lax.broadcasted_iota(jnp.int32, gate.shape, 1)
    ge = jnp.sum(jnp.where(lane == e, gate, 0.0), axis=1, keepdims=True)
    h = h_ref[...]
    for c0 in range(0, d_exp, f_chunk):
        c1 = c0 + f_chunk
        gl = _dot(h, win_ref[0, :, c0:c1]) + bin_ref[0, :, c0:c1]
        up = _dot(h, win_ref[0, :, d_exp + c0:d_exp + c1]) + bin_ref[0, :, d_exp + c0:d_exp + c1]
        gl = jnp.minimum(gl, SWIGLU_LIMIT)
        up = jnp.clip(up, -SWIGLU_LIMIT, SWIGLU_LIMIT)
        act = gl * jax.nn.sigmoid(SWIGLU_ALPHA * gl) * (up + 1.0) * ge
        acc_ref[...] += _dot(act.astype(BF16), wdn_ref[0, c0:c1, :])

    @pl.when(e == pl.num_programs(1) - 1)
    def _():
        f_ref[...] = acc_ref[...]


def _moe(h2, gate, w, tm):
    n, d = h2.shape
    n_exp, _, two_f = w["win"].shape
    d_exp = two_f // 2
    assert n % tm == 0
    return pl.pallas_call(
        functools.partial(_moe_kernel, f_chunk=min(512, d_exp)),
        grid=(n // tm, n_exp),
        in_specs=[pl.BlockSpec((tm, d), lambda i, e: (i, 0)),
                  pl.BlockSpec((tm, n_exp), lambda i, e: (i, 0)),
                  pl.BlockSpec((1, d, two_f), lambda i, e: (e, 0, 0)),
                  pl.BlockSpec((1, 1, two_f), lambda i, e: (e, 0, 0)),
                  pl.BlockSpec((1, d_exp, d), lambda i, e: (e, 0, 0)),
                  pl.BlockSpec((n_exp, d), lambda i, e: (0, 0))],
        out_specs=pl.BlockSpec((tm, d), lambda i, e: (i, 0)),
        out_shape=jax.ShapeDtypeStruct((n, d), F32),
        scratch_shapes=[pltpu.VMEM((tm, d), F32)],
        compiler_params=_params(("parallel", "arbitrary")),
        name="moe",
    )(h2, gate, w["win"], w["bin"], w["wdn"], w["bdn"])


def _final_kernel(x1_ref, f_ref, mod_ref, g_ref, y_ref, *, normalize):
    y = x1_ref[0] + mod_ref[0][5:6] * f_ref[0]
    y_ref[0] = _rms(y, g_ref[...]) if normalize else y


def _final(x1, f, mod, g, tt, normalize):
    b, s, d = x1.shape
    tok = pl.BlockSpec((1, tt, d), lambda bi, i: (bi, i, 0))
    return pl.pallas_call(
        functools.partial(_final_kernel, normalize=normalize),
        grid=(b, s // tt),
        in_specs=[tok, tok, pl.BlockSpec((1, 6, d), lambda bi, i: (bi, 0, 0)),
                  pl.BlockSpec((1, d), lambda bi, i: (0, 0))],
        out_specs=tok,
        out_shape=jax.ShapeDtypeStruct((b, s, d), F32),
        compiler_params=_params(("parallel", "parallel")),
        name="final",
    )(x1, f, mod, g)


def _layer_weights(l, w_in, q_norm_g, w_uq, kv_norm_g, w_ukv, norm1_g, subln_g, w_o_mla, w_o_diff, w_out,
                   norm2_g, w_router, b_router, w_e_in, b_e_in, w_e_down, b_e_down):
    d = w_in.shape[1]
    hw = HEADS * LANES
    sizes = (MLA_Q_RANK, MLA_KV_RANK, MLA_ROPE, hw, hw, hw, 2 * d)
    offs = np.concatenate([[0], np.cumsum(sizes)])
    pieces = [w_in[l][:, offs[i]:offs[i + 1]] for i in range(len(sizes))]
    w_cq, w_ckv, w_kr, w_dq, w_dk, w_dv, w_g = pieces
    half = MLA_ROPE // 2
    pad_lo = jnp.zeros((d, ROPE_LO), F32)
    pad_hi = jnp.zeros((d, LANES - ROPE_HI), F32)
    kr_lanes = jnp.concatenate([pad_lo, w_kr, pad_hi], axis=1)
    kr_swap = jnp.concatenate([pad_lo, w_kr[:, half:], w_kr[:, :half], pad_hi], axis=1)
    w1 = jnp.concatenate([w_cq, w_ckv, kr_lanes, kr_swap], axis=1)

    uq = w_uq[l].reshape(MLA_Q_RANK, HEADS, MLA_NOPE + MLA_ROPE)
    nope, rope = uq[..., :MLA_NOPE], uq[..., MLA_NOPE:]
    zq = lambda n: jnp.zeros((MLA_Q_RANK, HEADS, n), F32)
    wq = jnp.concatenate([nope, rope, zq(LANES - ROPE_HI)], axis=-1).reshape(MLA_Q_RANK, hw)
    wqs = jnp.concatenate([zq(ROPE_LO), rope[..., half:], rope[..., :half], zq(LANES - ROPE_HI)],
                          axis=-1).reshape(MLA_Q_RANK, hw)
    ukv = w_ukv[l].reshape(MLA_KV_RANK, HEADS, MLA_NOPE + MLA_V)
    zk = lambda n: jnp.zeros((MLA_KV_RANK, HEADS, n), F32)
    wk = jnp.concatenate([ukv[..., :MLA_NOPE], zk(LANES - MLA_NOPE)], axis=-1).reshape(MLA_KV_RANK, hw)
    wv = jnp.concatenate([ukv[..., MLA_NOPE:], zk(LANES - MLA_V)], axis=-1).reshape(MLA_KV_RANK, hw)
    wom = jnp.concatenate([w_o_mla[l].reshape(HEADS, MLA_V, d), jnp.zeros((HEADS, LANES - MLA_V, d), F32)],
                          axis=1).reshape(hw, d)
    wrh, wrl = _split_bf16(w_router[l])
    bf = lambda a: a.astype(BF16)
    row = lambda a: a.reshape(1, -1)
    return dict(
        n1=row(norm1_g[l]), qg=row(q_norm_g[l]), kvg=row(kv_norm_g[l]),
        w1=bf(w1), wq=bf(wq), wqs=bf(wqs), wk=bf(wk), wv=bf(wv),
        wdq=bf(w_dq), wdk=bf(w_dk), wdv=bf(w_dv), wg=bf(w_g),
        sg=row(subln_g[l]), wom=bf(wom), wod=bf(w_o_diff[l]), wout=bf(w_out[l]),
        n2=row(norm2_g[l]), wrh=wrh, wrl=wrl, br=row(b_router[l]),
        win=bf(w_e_in[l]), bin=b_e_in[l][:, None, :], wdn=bf(w_e_down[l]), bdn=bf(b_e_down[l]),
    )


def _tile(n, target):
    t = min(n, target)
    while n % t:
        t //= 2
    return t


def kernel(x_prompt, x_sample, cache_mla_latent, cache_mla_krope, cache_diff_k, cache_diff_v,
           c_prompt, c_sample, rel_table, final_norm_g, w_ada, b_ada, norm1_g, w_in, q_norm_g, w_uq,
           kv_norm_g, w_ukv, lam_q1, lam_k1, lam_q2, lam_k2, subln_g, w_o_mla, w_o_diff, w_out,
           norm2_g, w_router, b_router, w_e_in, b_e_in, w_e_down, b_e_down):
    depth = w_ada.shape[0]
    bp, sp, d = x_prompt.shape
    bs, ss, _ = x_sample.shape
    past = cache_mla_latent.shape[2]
    assert HEADS * DIFF_VD == HEADS * LANES and past % ss == 0 and ss % SUBLANES_BF16 == 0

    n_pos = max(sp, past + ss)
    rope_rows = _tile(n_pos, 2048)
    cos_tab, sin_tab = _rope_tables(-(-n_pos // rope_rows) * rope_rows, rope_rows)

    pad_rows = -(bp + bs) % SUBLANES_BF16
    c_all = jnp.concatenate([c_prompt, c_sample, jnp.zeros((pad_rows, d), F32)], axis=0)

    g_fin = final_norm_g.reshape(1, d)
    hp, hs = x_prompt, x_sample
    rows_p, rows_s = [], []
    for l in range(depth):
        lam_init = 0.8 - 0.6 * math.exp(-0.3 * l)
        w = _layer_weights(l, w_in, q_norm_g, w_uq, kv_norm_g, w_ukv, norm1_g, subln_g, w_o_mla, w_o_diff,
                           w_out, norm2_g, w_router, b_router, w_e_in, b_e_in, w_e_down, b_e_down)
        lam_vecs = jnp.stack([lam_q1[l], lam_k1[l], lam_q2[l], lam_k2[l]])
        mod, lamv = _ada(c_all, w_ada[l].astype(BF16), b_ada[l].reshape(1, -1), lam_vecs, lam_init)
        mod_p = mod[:bp].reshape(bp, 6, d)
        mod_s = mod[bp:bp + bs].reshape(bs, 6, d)
        post_scale = 1.0 - lam_init

        tt = _tile(sp, 256)
        lat, kr, dkf, dvf, qm, km, vm, qd, kd, vd, g = _proj(hp, mod_p, cos_tab, sin_tab, 0, tt, w)
        om, od = _attn_prompt(rel_table, qm, qd, km, vm, kd, vd, lamv, w["sg"], _tile(sp, 256), post_scale)
        x1, h2, gate = _post(hp, om, od, g, mod_p, tt, w)
        f = _moe(h2.reshape(bp * sp, d), gate.reshape(bp * sp, N_EXPERTS), w, _tile(bp * sp, 1024))
        hp = (x1, f.reshape(bp, sp, d), mod_p)
        rows_p.append((lat, kr, dkf.reshape(bp, sp, HEADS, 2 * DIFF_HD), dvf.reshape(bp, sp, HEADS, DIFF_VD)))

        lat, kr, dkf, dvf, qm, km, vm, qd, kd, vd, g = _proj(hs, mod_s, cos_tab, sin_tab, past, ss, w)
        om, od = _attn_sample(rel_table, qm, qd, km, vm, kd, vd,
                              cache_mla_latent[l], cache_mla_krope[l],
                              cache_diff_k[l].reshape(bs, past, HEADS * 2 * DIFF_HD),
                              cache_diff_v[l].reshape(bs, past, HEADS * DIFF_VD),
                              w["wk"], w["wv"], lamv, w["sg"], post_scale)
        x1, h2, gate = _post(hs, om, od, g, mod_s, ss, w)
        f = _moe(h2.reshape(bs * ss, d), gate.reshape(bs * ss, N_EXPERTS), w, bs * ss)
        hs = (x1, f.reshape(bs, ss, d), mod_s)
        rows_s.append((lat, kr, dkf.reshape(bs, ss, HEADS, 2 * DIFF_HD), dvf.reshape(bs, ss, HEADS, DIFF_VD)))

        if l + 1 < depth:
            hp = _final(*hp, g_fin, _tile(sp, 1024), normalize=False)
            hs = _final(*hs, g_fin, ss, normalize=False)

    y_prompt = _final(*hp, g_fin, _tile(sp, 1024), normalize=True)
    y_sample = _final(*hs, g_fin, ss, normalize=True)
    stack = lambda rows, i: jnp.stack([r[i] for r in rows])
    return (y_prompt, y_sample,
            stack(rows_p, 0), stack(rows_p, 1), stack(rows_p, 2), stack(rows_p, 3),
            stack(rows_s, 0), stack(rows_s, 1), stack(rows_s, 2), stack(rows_s, 3))
```

```python
import functools
import math

import jax
import jax.numpy as jnp
import numpy as np
from jax import lax
from jax.experimental import pallas as pl
from jax.experimental.pallas import tpu as pltpu

F32 = jnp.float32
BF16 = jnp.bfloat16

CHUNK = 64
EPS = 1e-6
HEADS = 8
MLA_Q_RANK = 384
MLA_KV_RANK = 256
MLA_NOPE = 64
MLA_ROPE = 32
MLA_V = 64
MLA_SCALE = (MLA_NOPE + MLA_ROPE) ** -0.5
ROPE_THETA = 10000.0
DIFF_HD = 64
DIFF_VD = 2 * DIFF_HD
DIFF_SCALE = DIFF_HD ** -0.5
REL_BUCKETS = 32
REL_MAX_DIST = 128
N_EXPERTS = 32
TOP_K = 4
SWIGLU_LIMIT = 7.0
SWIGLU_ALPHA = 1.702
NEG = -1e30
LOG2E = math.log2(math.e)

LANES = 128
SUBLANES_BF16 = 16
VMEM_LIMIT = 56 * 1024 * 1024

ROPE_LO = MLA_NOPE
ROPE_HI = MLA_NOPE + MLA_ROPE
CHUNK_SHIFT = CHUNK.bit_length() - 1
assert 1 << CHUNK_SHIFT == CHUNK

REL_HALF = REL_BUCKETS // 2
REL_EXACT = REL_HALF // 2
assert (REL_MAX_DIST // REL_EXACT) == 1 << (REL_HALF - REL_EXACT) // 2
REL_FAR = math.isqrt(REL_EXACT * REL_EXACT * 2 ** (REL_HALF - REL_EXACT - 1) - 1) + 1


def _params(sem, vmem=VMEM_LIMIT):
    return pltpu.CompilerParams(dimension_semantics=sem, vmem_limit_bytes=vmem)


def _const_spec(shape):
    zeros = (0,) * len(shape)
    return pl.BlockSpec(shape, lambda *_: zeros, pipeline_mode=pl.Buffered(1))


def _dot(a, b):
    return jnp.dot(a, b, preferred_element_type=F32)


def _dot_nt(a, b):
    return lax.dot_general(a, b, (((1,), (1,)), ((), ())), preferred_element_type=F32)


def _rms(x, g):
    return x * lax.rsqrt(jnp.mean(x * x, axis=-1, keepdims=True) + EPS) * g


def _split_bf16(x):
    hi = x.astype(BF16)
    lo = (x - hi.astype(F32)).astype(BF16)
    return hi, lo


def _rel_bucket(rel):
    n = jnp.abs(rel)
    n2 = n * n
    cnt = jnp.zeros_like(n)
    for j in range(1, REL_HALF - REL_EXACT):
        cnt = cnt + (n2 >= REL_EXACT * REL_EXACT * 2 ** j).astype(jnp.int32)
    large = REL_EXACT + cnt
    return jnp.where(rel > 0, REL_HALF, 0) + jnp.where(n < REL_EXACT, n, large)


def _bias_from_table(bucket, tab_ref, h):
    out = jnp.zeros(bucket.shape, F32)
    for b in range(REL_BUCKETS):
        out = jnp.where(bucket == b, tab_ref[b, h], out)
    return out


def _ada_kernel(c_ref, w_ref, b_ref, lamv_ref, mod_ref, lam_ref, *, lam_init):
    c = c_ref[...]
    s = (c * jax.nn.sigmoid(c)).astype(BF16)
    mod_ref[...] = _dot(s, w_ref[...]) + b_ref[...]

    @pl.when(pl.program_id(0) == 0)
    def _():
        v = lamv_ref[...]
        a = jnp.sum(v[0:1] * v[1:2], axis=1, keepdims=True)
        b = jnp.sum(v[2:3] * v[3:4], axis=1, keepdims=True)
        lam_ref[...] = jnp.broadcast_to(jnp.exp(a) - jnp.exp(b) + lam_init, lam_ref.shape)


def _ada(c_all, w_ada, b_ada, lam_vecs, lam_init):
    rows, d = c_all.shape
    n = w_ada.shape[1]
    tn = d
    return pl.pallas_call(
        functools.partial(_ada_kernel, lam_init=lam_init),
        grid=(n // tn,),
        in_specs=[
            pl.BlockSpec((rows, d), lambda j: (0, 0)),
            pl.BlockSpec((d, tn), lambda j: (0, j)),
            pl.BlockSpec((1, tn), lambda j: (0, j)),
            pl.BlockSpec(lam_vecs.shape, lambda j: (0, 0)),
        ],
        out_specs=[
            pl.BlockSpec((rows, tn), lambda j: (0, j)),
            pl.BlockSpec((8, LANES), lambda j: (0, 0)),
        ],
        out_shape=[
            jax.ShapeDtypeStruct((rows, n), F32),
            jax.ShapeDtypeStruct((8, LANES), F32),
        ],
        compiler_params=_params(("arbitrary",)),
        name="ada",
    )(c_all, w_ada, b_ada, lam_vecs)


def _rope_kernel(inv_ref, cos_ref, sin_ref):
    rows = cos_ref.shape[0]
    pos = pl.program_id(0) * rows + lax.broadcasted_iota(jnp.int32, (rows, LANES), 0)
    lane = lax.broadcasted_iota(jnp.int32, (rows, LANES), 1)
    ang = pos.astype(F32) * inv_ref[...]
    in_rope = (lane >= ROPE_LO) & (lane < ROPE_HI)
    first_half = lane < ROPE_LO + MLA_ROPE // 2
    cos_ref[...] = jnp.where(lane < ROPE_LO, 1.0, jnp.where(in_rope, jnp.cos(ang), 0.0))
    sn = jnp.sin(ang)
    sin_ref[...] = jnp.where(in_rope, jnp.where(first_half, -sn, sn), 0.0)


def _rope_tables(n_pos, rows):
    half = MLA_ROPE // 2
    inv = ROPE_THETA ** (-jnp.arange(half, dtype=F32) / half)
    inv_lanes = jnp.zeros((1, LANES), F32).at[0, ROPE_LO:ROPE_HI].set(jnp.concatenate([inv, inv]))
    return pl.pallas_call(
        _rope_kernel,
        grid=(n_pos // rows,),
        in_specs=[pl.BlockSpec((1, LANES), lambda i: (0, 0))],
        out_specs=[pl.BlockSpec((rows, LANES), lambda i: (i, 0))] * 2,
        out_shape=[jax.ShapeDtypeStruct((n_pos, LANES), F32)] * 2,
        compiler_params=_params(("parallel",)),
        name="rope_tables",
    )(inv_lanes)


def _proj_kernel(x_ref, mod_ref, cos_ref, sin_ref, n1_ref, qg_ref, kvg_ref,
                 w1_ref, wq_ref, wqs_ref, wk_ref, wv_ref, wdq_ref, wdk_ref, wdv_ref, wg_ref,
                 lat_ref, kr_ref, dkf_ref, dvf_ref,
                 qm_ref, km_ref, vm_ref, qd_ref, kd_ref, vd_ref, g_ref):
    m = mod_ref[0]
    h = (_rms(x_ref[0], n1_ref[...]) * (1.0 + m[1:2]) + m[0:1]).astype(BF16)
    cos = cos_ref[...]
    sin = sin_ref[...]

    z1 = _dot(h, w1_ref[...])
    r0 = MLA_Q_RANK
    r1 = r0 + MLA_KV_RANK
    cq, ckv = z1[:, :r0], z1[:, r0:r1]
    kr, krs = z1[:, r1:r1 + LANES], z1[:, r1 + LANES:]

    qc = _rms(cq, qg_ref[...]).astype(BF16)
    q = _dot(qc, wq_ref[...])
    qs = _dot(qc, wqs_ref[...])
    lat = _rms(ckv, kvg_ref[...])
    lat_ref[0] = lat
    latb = lat.astype(BF16)
    krope = kr * cos + krs * sin
    kr_ref[0] = krope[:, ROPE_LO:ROPE_HI]
    kn = _dot(latb, wk_ref[...])
    v = _dot(latb, wv_ref[...])
    dq = _dot(h, wdq_ref[...]) * (DIFF_SCALE * LOG2E)
    dk = _dot(h, wdk_ref[...])
    dv = _dot(h, wdv_ref[...])
    dkf_ref[0] = dk
    dvf_ref[0] = dv
    for hd in range(HEADS):
        sl = slice(hd * LANES, (hd + 1) * LANES)
        qm_ref[0, hd] = ((q[:, sl] * cos + qs[:, sl] * sin) * (MLA_SCALE * LOG2E)).astype(BF16)
        km_ref[0, hd] = (kn[:, sl] + krope).astype(BF16)
        vm_ref[0, hd] = v[:, sl].astype(BF16)
        qd_ref[0, hd] = dq[:, sl].astype(BF16)
        kd_ref[0, hd] = dk[:, sl].astype(BF16)
        vd_ref[0, hd] = dv[:, sl].astype(BF16)
    g_ref[0] = jax.nn.sigmoid(_dot(h, wg_ref[...])).astype(BF16)


def _proj(x, mod, cos_tab, sin_tab, pos0, tt, w):
    b, s, d = x.shape
    assert s % tt == 0 and pos0 % tt == 0
    off = pos0 // tt
    tok = lambda width: pl.BlockSpec((1, tt, width), lambda bi, i: (bi, i, 0))
    head = pl.BlockSpec((1, HEADS, tt, LANES), lambda bi, i: (bi, 0, i, 0))
    tab = pl.BlockSpec((tt, LANES), lambda bi, i: (off + i, 0))
    hm = jax.ShapeDtypeStruct((b, HEADS, s, LANES), BF16)
    weights = [w["n1"], w["qg"], w["kvg"], w["w1"], w["wq"], w["wqs"], w["wk"], w["wv"],
               w["wdq"], w["wdk"], w["wdv"], w["wg"]]
    return pl.pallas_call(
        _proj_kernel,
        grid=(b, s // tt),
        in_specs=[tok(d), pl.BlockSpec((1, 6, d), lambda bi, i: (bi, 0, 0)), tab, tab]
                 + [_const_spec(a.shape) for a in weights],
        out_specs=[tok(MLA_KV_RANK), tok(MLA_ROPE), tok(HEADS * LANES), tok(HEADS * LANES)]
                  + [head] * 6 + [tok(2 * d)],
        out_shape=[jax.ShapeDtypeStruct((b, s, MLA_KV_RANK), F32),
                   jax.ShapeDtypeStruct((b, s, MLA_ROPE), F32),
                   jax.ShapeDtypeStruct((b, s, HEADS * LANES), F32),
                   jax.ShapeDtypeStruct((b, s, HEADS * LANES), F32)]
                  + [hm] * 6 + [jax.ShapeDtypeStruct((b, s, 2 * d), BF16)],
        compiler_params=_params(("parallel", "parallel")),
        name="proj",
    )(x, mod, cos_tab, sin_tab, *weights)


def _online_softmax(s, v, m_ref, l_ref, acc_ref, h):
    m_prev = m_ref[h]
    m_new = jnp.maximum(m_prev, jnp.max(s, axis=1, keepdims=True))
    alpha = jnp.exp2(m_prev - m_new)
    p = jnp.exp2(s - jnp.tile(m_new, (1, s.shape[1] // LANES)))
    l_ref[h] = alpha * l_ref[h] + jnp.sum(p, axis=1, keepdims=True)
    acc_ref[h] = alpha * acc_ref[h] + _dot(p.astype(BF16), v)
    m_ref[h] = m_new


def _diff_queries(qd):
    lane = lax.broadcasted_iota(jnp.int32, (1, LANES), 1)
    first = (lane < DIFF_HD).astype(qd.dtype)
    return jnp.concatenate([qd * first, qd * (1 - first)], axis=0)


def _diff_output(o12, lam, sg, post_scale):
    rows = o12.shape[0] // 2
    o = o12[:rows] - lam * o12[rows:]
    return _rms(o, sg) * post_scale


def _attn_kernel(qi_ref, kj_ref, cls_ref, tab_ref, qm_ref, qd_ref, km_ref, vm_ref, kd_ref, vd_ref, lam_ref, sg_ref,
                 om_ref, od_ref,
                 bias_s, mask_s, m_m, l_m, acc_m, m_d, l_d, acc_d, *, post_scale):
    tq = qm_ref.shape[2]
    tk = km_ref.shape[2]
    n_diag = tk // tq
    p = pl.program_id(1)
    kj = kj_ref[p]
    cls = cls_ref[p]

    @pl.when((pl.program_id(0) == 0) & (p == 0))
    def _build_bias():
        row = lax.broadcasted_iota(jnp.int32, (tq, tk), 0)
        col = lax.broadcasted_iota(jnp.int32, (tq, tk), 1)
        for near in range(n_diag + 1):
            qrow = row + (near * tq if near < n_diag else tk)
            bucket = _rel_bucket(col - qrow)
            masked = (col >> CHUNK_SHIFT) > (qrow >> CHUNK_SHIFT)
            if near < n_diag:
                mask_s[near] = jnp.where(masked, NEG, 0.0)

            def per_head(h, carry):
                far = tab_ref[REL_HALF - 1, h]
                b = (_bias_from_table(bucket, tab_ref, h) - far) * LOG2E
                bias_s[near, h] = jnp.where(masked, NEG, b) if near < n_diag else b
                return carry

            lax.fori_loop(0, HEADS, per_head, 0)

    @pl.when(kj == 0)
    def _init():
        m_m[...] = jnp.full(m_m.shape, NEG, F32)
        m_d[...] = jnp.full(m_d.shape, NEG, F32)
        l_m[...] = jnp.zeros(l_m.shape, F32)
        l_d[...] = jnp.zeros(l_d.shape, F32)
        acc_m[...] = jnp.zeros(acc_m.shape, F32)
        acc_d[...] = jnp.zeros(acc_d.shape, F32)

    def step(near):
        keys = (near + 1) * tq if near is not None and near < n_diag else tk
        for h in range(HEADS):
            s = _dot_nt(qm_ref[0, h], km_ref[0, h, :keys, :])
            if near is not None and near < n_diag:
                s = s + mask_s[near, :, :keys]
            _online_softmax(s, vm_ref[0, h, :keys, :], m_m, l_m, acc_m, h)
            s12 = _dot_nt(_diff_queries(qd_ref[0, h]), kd_ref[0, h, :keys, :])
            if near is not None:
                b = bias_s[near, h, :, :keys]
                s12 = s12 + jnp.concatenate([b, b], axis=0)
            _online_softmax(s12, vd_ref[0, h, :keys, :], m_d, l_d, acc_d, h)

    @pl.when(cls == n_diag + 1)
    def _far():
        step(None)

    @pl.when(cls == n_diag)
    def _before_diagonal():
        step(n_diag)

    for r in range(n_diag):
        @pl.when(cls == r)
        def _diagonal():
            step(r)
            lam = lam_ref[0:1, :]
            sg = sg_ref[...]

            def finish(h, carry):
                om_ref[0, h] = (acc_m[h] / l_m[h]).astype(om_ref.dtype)
                od_ref[0, h] = _diff_output(acc_d[h] / l_d[h], lam, sg, post_scale).astype(od_ref.dtype)
                return carry

            lax.fori_loop(0, HEADS, finish, 0)


def _attn_prompt(rel_table, qm, qd, km, vm, kd, vd, lamv, sg, tq, tk, post_scale):
    b, _, s, _ = qm.shape
    assert s % tk == 0 and tk % tq == 0 and tq % CHUNK == 0 and tq % LANES == 0 and tq + 1 >= REL_FAR
    n_diag = tk // tq
    qi, kj, cls = [], [], []
    for i in range(s // tq):
        jd, r = divmod(i, n_diag)
        for j in range(jd + 1):
            qi.append(i)
            kj.append(j)
            cls.append(r if j == jd else n_diag if (j == jd - 1 and r == 0) else n_diag + 1)
    tables = [jnp.asarray(np.array(a, np.int32)) for a in (qi, kj, cls)]
    qspec = pl.BlockSpec((1, HEADS, tq, LANES), lambda bi, p, qi_r, kj_r, c_r: (bi, 0, qi_r[p], 0))
    kspec = pl.BlockSpec((1, HEADS, tk, LANES), lambda bi, p, qi_r, kj_r, c_r: (bi, 0, kj_r[p], 0))
    stat = lambda rows: pltpu.VMEM((HEADS, rows, LANES), F32)
    out = jax.ShapeDtypeStruct((b, HEADS, s, LANES), BF16)
    return pl.pallas_call(
        functools.partial(_attn_kernel, post_scale=post_scale),
        grid_spec=pltpu.PrefetchScalarGridSpec(
            num_scalar_prefetch=3,
            grid=(b, len(qi)),
            in_specs=[pl.BlockSpec(memory_space=pltpu.SMEM),
                      qspec, qspec, kspec, kspec, kspec, kspec,
                      pl.BlockSpec((8, LANES), lambda bi, p, qi_r, kj_r, c_r: (0, 0)),
                      pl.BlockSpec((1, LANES), lambda bi, p, qi_r, kj_r, c_r: (0, 0))],
            out_specs=[qspec, qspec],
            scratch_shapes=[pltpu.VMEM((n_diag + 1, HEADS, tq, tk), F32), pltpu.VMEM((n_diag, tq, tk), F32),
                            stat(tq), stat(tq), stat(tq), stat(2 * tq), stat(2 * tq), stat(2 * tq)],
        ),
        out_shape=[out, out],
        compiler_params=_params(("arbitrary", "arbitrary")),
        name="attn_prompt",
    )(*tables, rel_table, qm, qd, km, vm, kd, vd, lamv, sg)


def _attn_sample_kernel(tab_ref, qm_ref, qd_ref, kmn_ref, vmn_ref, kdn_ref, vdn_ref,
                        latc_ref, krc_ref, kdc_ref, vdc_ref, wk_ref, wv_ref, lam_ref, sg_ref,
                        om_ref, od_ref, *, post_scale):
    h = pl.program_id(1)
    nq = qm_ref.shape[2]
    past = latc_ref.shape[1]

    latb = latc_ref[0].astype(BF16)
    r = lax.broadcasted_iota(jnp.int32, (MLA_ROPE, LANES), 0)
    c = lax.broadcasted_iota(jnp.int32, (MLA_ROPE, LANES), 1)
    place = (c == r + ROPE_LO).astype(BF16)
    kc = (_dot(latb, wk_ref[...]) + _dot(krc_ref[0].astype(BF16), place)).astype(BF16)
    vc = _dot(latb, wv_ref[...]).astype(BF16)

    def geometry(reps, n_keys, key0):
        row = lax.broadcasted_iota(jnp.int32, (reps * nq, n_keys), 0)
        qpos = past + jnp.where(row >= nq, row - nq, row)
        kpos = key0 + lax.broadcasted_iota(jnp.int32, (reps * nq, n_keys), 1)
        visible = (kpos >> CHUNK_SHIFT) <= (qpos >> CHUNK_SHIFT)
        return visible, kpos - qpos

    def softmax_pv(s_c, s_n, v_c, v_n):
        m = jnp.maximum(jnp.max(s_c, axis=1, keepdims=True), jnp.max(s_n, axis=1, keepdims=True))
        p_c = jnp.exp2(s_c - m)
        p_n = jnp.exp2(s_n - m)
        l = jnp.sum(p_c, axis=1, keepdims=True) + jnp.sum(p_n, axis=1, keepdims=True)
        return (_dot(p_c.astype(BF16), v_c) + _dot(p_n.astype(BF16), v_n)) / l

    q = qm_ref[0, 0]
    vis_c, _ = geometry(1, past, 0)
    vis_n, _ = geometry(1, nq, past)
    s_c = jnp.where(vis_c, _dot_nt(q, kc), NEG)
    s_n = jnp.where(vis_n, _dot_nt(q, kmn_ref[0, 0]), NEG)
    om_ref[0, 0] = softmax_pv(s_c, s_n, vc, vmn_ref[0, 0]).astype(om_ref.dtype)

    q12 = _diff_queries(qd_ref[0, 0])
    vis_c, rel_c = geometry(2, past, 0)
    vis_n, rel_n = geometry(2, nq, past)
    b_c = _bias_from_table(_rel_bucket(rel_c), tab_ref, h) * LOG2E
    b_n = _bias_from_table(_rel_bucket(rel_n), tab_ref, h) * LOG2E
    s_c = jnp.where(vis_c, _dot_nt(q12, kdc_ref[0].astype(BF16)) + b_c, NEG)
    s_n = jnp.where(vis_n, _dot_nt(q12, kdn_ref[0, 0]) + b_n, NEG)
    o12 = softmax_pv(s_c, s_n, vdc_ref[0].astype(BF16), vdn_ref[0, 0])
    od_ref[0, 0] = _diff_output(o12, lam_ref[0:1, :], sg_ref[...], post_scale).astype(od_ref.dtype)


def _attn_sample(rel_table, qm, qd, kmn, vmn, kdn, vdn, latc, krc, kdc, vdc, wk, wv, lamv, sg, post_scale):
    b, _, nq, _ = qm.shape
    past = latc.shape[1]
    new = pl.BlockSpec((1, 1, nq, LANES), lambda bi, h: (bi, h, 0, 0))
    cache_head = pl.BlockSpec((1, past, LANES), lambda bi, h: (bi, 0, h))
    out = jax.ShapeDtypeStruct((b, HEADS, nq, LANES), BF16)
    return pl.pallas_call(
        functools.partial(_attn_sample_kernel, post_scale=post_scale),
        grid=(b, HEADS),
        in_specs=[pl.BlockSpec(memory_space=pltpu.SMEM), new, new, new, new, new, new,
                  pl.BlockSpec((1, past, MLA_KV_RANK), lambda bi, h: (bi, 0, 0)),
                  pl.BlockSpec((1, past, MLA_ROPE), lambda bi, h: (bi, 0, 0)),
                  cache_head, cache_head,
                  pl.BlockSpec((MLA_KV_RANK, LANES), lambda bi, h: (0, h)),
                  pl.BlockSpec((MLA_KV_RANK, LANES), lambda bi, h: (0, h)),
                  pl.BlockSpec((8, LANES), lambda bi, h: (0, 0)),
                  pl.BlockSpec((1, LANES), lambda bi, h: (0, 0))],
        out_specs=[new, new],
        out_shape=[out, out],
        compiler_params=_params(("parallel", "parallel")),
        name="attn_sample",
    )(rel_table, qm, qd, kmn, vmn, kdn, vdn, latc, krc, kdc, vdc, wk, wv, lamv, sg)


def _post_kernel(x_ref, om_ref, od_ref, g_ref, mod_ref, wom_ref, wod_ref, wout_ref, n2_ref,
                 wrh_ref, wrl_ref, br_ref, x1_ref, h2_ref, gate_ref):
    d = x_ref.shape[2]
    om = jnp.concatenate([om_ref[0, hd] for hd in range(HEADS)], axis=1)
    od = jnp.concatenate([od_ref[0, hd] for hd in range(HEADS)], axis=1)
    g = g_ref[0].astype(F32)
    merged = g[:, :d] * _dot(om, wom_ref[...]) + g[:, d:] * _dot(od, wod_ref[...])
    m = mod_ref[0]
    x1 = x_ref[0] + m[2:3] * _dot(merged.astype(BF16), wout_ref[...])
    x1_ref[0] = x1
    h2 = _rms(x1, n2_ref[...]) * (1.0 + m[4:5]) + m[3:4]
    h2_ref[0] = h2.astype(BF16)

    hi, lo = _split_bf16(h2)
    wrh = wrh_ref[...]
    logits = _dot(hi, wrh) + (_dot(hi, wrl_ref[...]) + _dot(lo, wrh)) + br_ref[...]
    lane = lax.broadcasted_iota(jnp.int32, logits.shape, 1).astype(F32)
    work = logits
    tops, picks = [], []
    for _ in range(TOP_K):
        top = jnp.max(work, axis=1, keepdims=True)
        first = jnp.min(jnp.where(work == top, lane, float(N_EXPERTS)), axis=1, keepdims=True)
        pick = lane == first
        tops.append(top)
        picks.append(pick)
        work = jnp.where(pick, -jnp.inf, work)
    es = [jnp.exp(top - tops[0]) for top in tops]
    den = es[0]
    for e in es[1:]:
        den = den + e
    gate = jnp.zeros(logits.shape, F32)
    for e, pick in zip(es, picks):
        gate = jnp.where(pick, e / den, gate)
    gate_ref[0] = gate


def _post(x, om, od, g, mod, tt, w):
    b, s, d = x.shape
    tok = lambda width: pl.BlockSpec((1, tt, width), lambda bi, i: (bi, i, 0))
    head = pl.BlockSpec((1, HEADS, tt, LANES), lambda bi, i: (bi, 0, i, 0))
    weights = [w["wom"], w["wod"], w["wout"], w["n2"], w["wrh"], w["wrl"], w["br"]]
    return pl.pallas_call(
        _post_kernel,
        grid=(b, s // tt),
        in_specs=[tok(d), head, head, tok(2 * d), pl.BlockSpec((1, 6, d), lambda bi, i: (bi, 0, 0))]
                 + [_const_spec(a.shape) for a in weights],
        out_specs=[tok(d), tok(d), tok(N_EXPERTS)],
        out_shape=[jax.ShapeDtypeStruct((b, s, d), F32),
                   jax.ShapeDtypeStruct((b, s, d), BF16),
                   jax.ShapeDtypeStruct((b, s, N_EXPERTS), F32)],
        compiler_params=_params(("parallel", "parallel")),
        name="post",
    )(x, om, od, g, mod, *weights)


def _moe_kernel(h_ref, gate_ref, win_ref, bin_ref, wdn_ref, bdn_ref, f_ref, acc_ref, *, f_chunk):
    e = pl.program_id(1)
    gate = gate_ref[...]
    d_exp = wdn_ref.shape[1]

    @pl.when(e == 0)
    def _():
        hi, lo = _split_bf16(gate)
        bdn = bdn_ref[...]
        acc_ref[...] = _dot(hi, bdn) + _dot(lo, bdn)

    lane = lax.broadcasted_iota(jnp.int32, gate.shape, 1)
    ge = jnp.sum(jnp.where(lane == e, gate, 0.0), axis=1, keepdims=True)
    h = h_ref[...]
    for c0 in range(0, d_exp, f_chunk):
        c1 = c0 + f_chunk
        gl = _dot(h, win_ref[0, :, c0:c1]) + bin_ref[0, :, c0:c1]
        up = _dot(h, win_ref[0, :, d_exp + c0:d_exp + c1]) + bin_ref[0, :, d_exp + c0:d_exp + c1]
        gl = jnp.minimum(gl, SWIGLU_LIMIT)
        up = jnp.clip(up, -SWIGLU_LIMIT, SWIGLU_LIMIT)
        act = gl * jax.nn.sigmoid(SWIGLU_ALPHA * gl) * (up + 1.0) * ge
        acc_ref[...] += _dot(act.astype(BF16), wdn_ref[0, c0:c1, :])

    @pl.when(e == pl.num_programs(1) - 1)
    def _():
        f_ref[...] = acc_ref[...]


def _moe(h2, gate, w, tm):
    n, d = h2.shape
    n_exp, _, two_f = w["win"].shape
    d_exp = two_f // 2
    assert n % tm == 0
    return pl.pallas_call(
        functools.partial(_moe_kernel, f_chunk=min(512, d_exp)),
        grid=(n // tm, n_exp),
        in_specs=[pl.BlockSpec((tm, d), lambda i, e: (i, 0)),
                  pl.BlockSpec((tm, n_exp), lambda i, e: (i, 0)),
                  pl.BlockSpec((1, d, two_f), lambda i, e: (e, 0, 0)),
                  pl.BlockSpec((1, 1, two_f), lambda i, e: (e, 0, 0)),
                  pl.BlockSpec((1, d_exp, d), lambda i, e: (e, 0, 0)),
                  pl.BlockSpec((n_exp, d), lambda i, e: (0, 0))],
        out_specs=pl.BlockSpec((tm, d), lambda i, e: (i, 0)),
        out_shape=jax.ShapeDtypeStruct((n, d), F32),
        scratch_shapes=[pltpu.VMEM((tm, d), F32)],
        compiler_params=_params(("parallel", "arbitrary")),
        name="moe",
    )(h2, gate, w["win"], w["bin"], w["wdn"], w["bdn"])


def _final_kernel(x1_ref, f_ref, mod_ref, g_ref, y_ref, *, normalize):
    y = x1_ref[0] + mod_ref[0][5:6] * f_ref[0]
    y_ref[0] = _rms(y, g_ref[...]) if normalize else y


def _final(x1, f, mod, g, tt, normalize):
    b, s, d = x1.shape
    tok = pl.BlockSpec((1, tt, d), lambda bi, i: (bi, i, 0))
    return pl.pallas_call(
        functools.partial(_final_kernel, normalize=normalize),
        grid=(b, s // tt),
        in_specs=[tok, tok, pl.BlockSpec((1, 6, d), lambda bi, i: (bi, 0, 0)),
                  pl.BlockSpec((1, d), lambda bi, i: (0, 0))],
        out_specs=tok,
        out_shape=jax.ShapeDtypeStruct((b, s, d), F32),
        compiler_params=_params(("parallel", "parallel")),
        name="final",
    )(x1, f, mod, g)


def _layer_weights(l, w_in, q_norm_g, w_uq, kv_norm_g, w_ukv, norm1_g, subln_g, w_o_mla, w_o_diff, w_out,
                   norm2_g, w_router, b_router, w_e_in, b_e_in, w_e_down, b_e_down):
    d = w_in.shape[1]
    hw = HEADS * LANES
    sizes = (MLA_Q_RANK, MLA_KV_RANK, MLA_ROPE, hw, hw, hw, 2 * d)
    offs = np.concatenate([[0], np.cumsum(sizes)])
    pieces = [w_in[l][:, offs[i]:offs[i + 1]] for i in range(len(sizes))]
    w_cq, w_ckv, w_kr, w_dq, w_dk, w_dv, w_g = pieces
    half = MLA_ROPE // 2
    pad_lo = jnp.zeros((d, ROPE_LO), F32)
    pad_hi = jnp.zeros((d, LANES - ROPE_HI), F32)
    kr_lanes = jnp.concatenate([pad_lo, w_kr, pad_hi], axis=1)
    kr_swap = jnp.concatenate([pad_lo, w_kr[:, half:], w_kr[:, :half], pad_hi], axis=1)
    w1 = jnp.concatenate([w_cq, w_ckv, kr_lanes, kr_swap], axis=1)

    uq = w_uq[l].reshape(MLA_Q_RANK, HEADS, MLA_NOPE + MLA_ROPE)
    nope, rope = uq[..., :MLA_NOPE], uq[..., MLA_NOPE:]
    zq = lambda n: jnp.zeros((MLA_Q_RANK, HEADS, n), F32)
    wq = jnp.concatenate([nope, rope, zq(LANES - ROPE_HI)], axis=-1).reshape(MLA_Q_RANK, hw)
    wqs = jnp.concatenate([zq(ROPE_LO), rope[..., half:], rope[..., :half], zq(LANES - ROPE_HI)],
                          axis=-1).reshape(MLA_Q_RANK, hw)
    ukv = w_ukv[l].reshape(MLA_KV_RANK, HEADS, MLA_NOPE + MLA_V)
    zk = lambda n: jnp.zeros((MLA_KV_RANK, HEADS, n), F32)
    wk = jnp.concatenate([ukv[..., :MLA_NOPE], zk(LANES - MLA_NOPE)], axis=-1).reshape(MLA_KV_RANK, hw)
    wv = jnp.concatenate([ukv[..., MLA_NOPE:], zk(LANES - MLA_V)], axis=-1).reshape(MLA_KV_RANK, hw)
    wom = jnp.concatenate([w_o_mla[l].reshape(HEADS, MLA_V, d), jnp.zeros((HEADS, LANES - MLA_V, d), F32)],
                          axis=1).reshape(hw, d)
    wrh, wrl = _split_bf16(w_router[l])
    bf = lambda a: a.astype(BF16)
    row = lambda a: a.reshape(1, -1)
    return dict(
        n1=row(norm1_g[l]), qg=row(q_norm_g[l]), kvg=row(kv_norm_g[l]),
        w1=bf(w1), wq=bf(wq), wqs=bf(wqs), wk=bf(wk), wv=bf(wv),
        wdq=bf(w_dq), wdk=bf(w_dk), wdv=bf(w_dv), wg=bf(w_g),
        sg=row(subln_g[l]), wom=bf(wom), wod=bf(w_o_diff[l]), wout=bf(w_out[l]),
        n2=row(norm2_g[l]), wrh=wrh, wrl=wrl, br=row(b_router[l]),
        win=bf(w_e_in[l]), bin=b_e_in[l][:, None, :], wdn=bf(w_e_down[l]), bdn=bf(b_e_down[l]),
    )


def _tile(n, target):
    t = min(n, target)
    while n % t:
        t //= 2
    return t


def kernel(x_prompt, x_sample, cache_mla_latent, cache_mla_krope, cache_diff_k, cache_diff_v,
           c_prompt, c_sample, rel_table, final_norm_g, w_ada, b_ada, norm1_g, w_in, q_norm_g, w_uq,
           kv_norm_g, w_ukv, lam_q1, lam_k1, lam_q2, lam_k2, subln_g, w_o_mla, w_o_diff, w_out,
           norm2_g, w_router, b_router, w_e_in, b_e_in, w_e_down, b_e_down):
    depth = w_ada.shape[0]
    bp, sp, d = x_prompt.shape
    bs, ss, _ = x_sample.shape
    past = cache_mla_latent.shape[2]
    assert HEADS * DIFF_VD == HEADS * LANES and past % ss == 0 and ss % SUBLANES_BF16 == 0

    n_pos = max(sp, past + ss)
    rope_rows = _tile(n_pos, 2048)
    cos_tab, sin_tab = _rope_tables(-(-n_pos // rope_rows) * rope_rows, rope_rows)

    pad_rows = -(bp + bs) % SUBLANES_BF16
    c_all = jnp.concatenate([c_prompt, c_sample, jnp.zeros((pad_rows, d), F32)], axis=0)

    g_fin = final_norm_g.reshape(1, d)
    hp, hs = x_prompt, x_sample
    rows_p, rows_s = [], []
    for l in range(depth):
        lam_init = 0.8 - 0.6 * math.exp(-0.3 * l)
        w = _layer_weights(l, w_in, q_norm_g, w_uq, kv_norm_g, w_ukv, norm1_g, subln_g, w_o_mla, w_o_diff,
                           w_out, norm2_g, w_router, b_router, w_e_in, b_e_in, w_e_down, b_e_down)
        lam_vecs = jnp.stack([lam_q1[l], lam_k1[l], lam_q2[l], lam_k2[l]])
        mod, lamv = _ada(c_all, w_ada[l].astype(BF16), b_ada[l].reshape(1, -1), lam_vecs, lam_init)
        mod_p = mod[:bp].reshape(bp, 6, d)
        mod_s = mod[bp:bp + bs].reshape(bs, 6, d)
        post_scale = 1.0 - lam_init

        tt = _tile(sp, 256)
        lat, kr, dkf, dvf, qm, km, vm, qd, kd, vd, g = _proj(hp, mod_p, cos_tab, sin_tab, 0, tt, w)
        om, od = _attn_prompt(rel_table, qm, qd, km, vm, kd, vd, lamv, w["sg"], _tile(sp, 256), _tile(sp, 512), post_scale)
        x1, h2, gate = _post(hp, om, od, g, mod_p, tt, w)
        f = _moe(h2.reshape(bp * sp, d), gate.reshape(bp * sp, N_EXPERTS), w, _tile(bp * sp, 1024))
        hp = (x1, f.reshape(bp, sp, d), mod_p)
        rows_p.append((lat, kr, dkf.reshape(bp, sp, HEADS, 2 * DIFF_HD), dvf.reshape(bp, sp, HEADS, DIFF_VD)))

        lat, kr, dkf, dvf, qm, km, vm, qd, kd, vd, g = _proj(hs, mod_s, cos_tab, sin_tab, past, ss, w)
        om, od = _attn_sample(rel_table, qm, qd, km, vm, kd, vd,
                              cache_mla_latent[l], cache_mla_krope[l],
                              cache_diff_k[l].reshape(bs, past, HEADS * 2 * DIFF_HD),
                              cache_diff_v[l].reshape(bs, past, HEADS * DIFF_VD),
                              w["wk"], w["wv"], lamv, w["sg"], post_scale)
        x1, h2, gate = _post(hs, om, od, g, mod_s, ss, w)
        f = _moe(h2.reshape(bs * ss, d), gate.reshape(bs * ss, N_EXPERTS), w, bs * ss)
        hs = (x1, f.reshape(bs, ss, d), mod_s)
        rows_s.append((lat, kr, dkf.reshape(bs, ss, HEADS, 2 * DIFF_HD), dvf.reshape(bs, ss, HEADS, DIFF_VD)))

        if l + 1 < depth:
            hp = _final(*hp, g_fin, _tile(sp, 1024), normalize=False)
            hs = _final(*hs, g_fin, ss, normalize=False)

    y_prompt = _final(*hp, g_fin, _tile(sp, 1024), normalize=True)
    y_sample = _final(*hs, g_fin, ss, normalize=True)
    stack = lambda rows, i: jnp.stack([r[i] for r in rows])
    return (y_prompt, y_sample,
            stack(rows_p, 0), stack(rows_p, 1), stack(rows_p, 2), stack(rows_p, 3),
            stack(rows_s, 0), stack(rows_s, 1), stack(rows_s, 2), stack(rows_s, 3))
```

```python
import functools
import math

import jax
import jax.numpy as jnp
import numpy as np
from jax import lax
from jax.experimental import pallas as pl
from jax.experimental.pallas import tpu as pltpu

F32 = jnp.float32
BF16 = jnp.bfloat16

CHUNK = 64
EPS = 1e-6
HEADS = 8
MLA_Q_RANK = 384
MLA_KV_RANK = 256
MLA_NOPE = 64
MLA_ROPE = 32
MLA_V = 64
MLA_SCALE = (MLA_NOPE + MLA_ROPE) ** -0.5
ROPE_THETA = 10000.0
DIFF_HD = 64
DIFF_VD = 2 * DIFF_HD
DIFF_SCALE = DIFF_HD ** -0.5
REL_BUCKETS = 32
REL_MAX_DIST = 128
N_EXPERTS = 32
TOP_K = 4
SWIGLU_LIMIT = 7.0
SWIGLU_ALPHA = 1.702
NEG = -1e30
LOG2E = math.log2(math.e)

LANES = 128
SUBLANES_BF16 = 16
VMEM_LIMIT = 56 * 1024 * 1024

ROPE_LO = MLA_NOPE
ROPE_HI = MLA_NOPE + MLA_ROPE
CHUNK_SHIFT = CHUNK.bit_length() - 1
assert 1 << CHUNK_SHIFT == CHUNK

REL_HALF = REL_BUCKETS // 2
REL_EXACT = REL_HALF // 2
assert (REL_MAX_DIST // REL_EXACT) == 1 << (REL_HALF - REL_EXACT) // 2
REL_FAR = math.isqrt(REL_EXACT * REL_EXACT * 2 ** (REL_HALF - REL_EXACT - 1) - 1) + 1


def _params(sem, vmem=VMEM_LIMIT):
    return pltpu.CompilerParams(dimension_semantics=sem, vmem_limit_bytes=vmem)


def _const_spec(shape):
    zeros = (0,) * len(shape)
    return pl.BlockSpec(shape, lambda *_: zeros, pipeline_mode=pl.Buffered(1))


def _dot(a, b):
    return jnp.dot(a, b, preferred_element_type=F32)


def _dot_nt(a, b):
    return lax.dot_general(a, b, (((1,), (1,)), ((), ())), preferred_element_type=F32)


def _rms(x, g):
    return x * lax.rsqrt(jnp.mean(x * x, axis=-1, keepdims=True) + EPS) * g


def _split_bf16(x):
    hi = x.astype(BF16)
    lo = (x - hi.astype(F32)).astype(BF16)
    return hi, lo


def _rel_bucket(rel):
    n = jnp.abs(rel)
    n2 = n * n
    cnt = jnp.zeros_like(n)
    for j in range(1, REL_HALF - REL_EXACT):
        cnt = cnt + (n2 >= REL_EXACT * REL_EXACT * 2 ** j).astype(jnp.int32)
    large = REL_EXACT + cnt
    return jnp.where(rel > 0, REL_HALF, 0) + jnp.where(n < REL_EXACT, n, large)


def _bias_from_table(bucket, tab_ref, h):
    out = jnp.zeros(bucket.shape, F32)
    for b in range(REL_BUCKETS):
        out = jnp.where(bucket == b, tab_ref[b, h], out)
    return out


def _ada_kernel(c_ref, w_ref, b_ref, lamv_ref, mod_ref, lam_ref, *, lam_init):
    c = c_ref[...]
    s = (c * jax.nn.sigmoid(c)).astype(BF16)
    mod_ref[...] = _dot(s, w_ref[...]) + b_ref[...]

    @pl.when(pl.program_id(0) == 0)
    def _():
        v = lamv_ref[...]
        a = jnp.sum(v[0:1] * v[1:2], axis=1, keepdims=True)
        b = jnp.sum(v[2:3] * v[3:4], axis=1, keepdims=True)
        lam_ref[...] = jnp.broadcast_to(jnp.exp(a) - jnp.exp(b) + lam_init, lam_ref.shape)


def _ada(c_all, w_ada, b_ada, lam_vecs, lam_init):
    rows, d = c_all.shape
    n = w_ada.shape[1]
    tn = d
    return pl.pallas_call(
        functools.partial(_ada_kernel, lam_init=lam_init),
        grid=(n // tn,),
        in_specs=[
            pl.BlockSpec((rows, d), lambda j: (0, 0)),
            pl.BlockSpec((d, tn), lambda j: (0, j)),
            pl.BlockSpec((1, tn), lambda j: (0, j)),
            pl.BlockSpec(lam_vecs.shape, lambda j: (0, 0)),
        ],
        out_specs=[
            pl.BlockSpec((rows, tn), lambda j: (0, j)),
            pl.BlockSpec((8, LANES), lambda j: (0, 0)),
        ],
        out_shape=[
            jax.ShapeDtypeStruct((rows, n), F32),
            jax.ShapeDtypeStruct((8, LANES), F32),
        ],
        compiler_params=_params(("arbitrary",)),
        name="ada",
    )(c_all, w_ada, b_ada, lam_vecs)


def _rope_kernel(inv_ref, cos_ref, sin_ref):
    rows = cos_ref.shape[0]
    pos = pl.program_id(0) * rows + lax.broadcasted_iota(jnp.int32, (rows, LANES), 0)
    lane = lax.broadcasted_iota(jnp.int32, (rows, LANES), 1)
    ang = pos.astype(F32) * inv_ref[...]
    in_rope = (lane >= ROPE_LO) & (lane < ROPE_HI)
    first_half = lane < ROPE_LO + MLA_ROPE // 2
    cos_ref[...] = jnp.where(lane < ROPE_LO, 1.0, jnp.where(in_rope, jnp.cos(ang), 0.0))
    sn = jnp.sin(ang)
    sin_ref[...] = jnp.where(in_rope, jnp.where(first_half, -sn, sn), 0.0)


def _rope_tables(n_pos, rows):
    half = MLA_ROPE // 2
    inv = ROPE_THETA ** (-jnp.arange(half, dtype=F32) / half)
    inv_lanes = jnp.zeros((1, LANES), F32).at[0, ROPE_LO:ROPE_HI].set(jnp.concatenate([inv, inv]))
    return pl.pallas_call(
        _rope_kernel,
        grid=(n_pos // rows,),
        in_specs=[pl.BlockSpec((1, LANES), lambda i: (0, 0))],
        out_specs=[pl.BlockSpec((rows, LANES), lambda i: (i, 0))] * 2,
        out_shape=[jax.ShapeDtypeStruct((n_pos, LANES), F32)] * 2,
        compiler_params=_params(("parallel",)),
        name="rope_tables",
    )(inv_lanes)


def _proj_kernel(x_ref, mod_ref, cos_ref, sin_ref, n1_ref, qg_ref, kvg_ref,
                 w1_ref, wq_ref, wqs_ref, wk_ref, wv_ref, wdq_ref, wdk_ref, wdv_ref, wg_ref,
                 lat_ref, kr_ref, dkf_ref, dvf_ref,
                 qm_ref, km_ref, vm_ref, qd_ref, kd_ref, vd_ref, g_ref):
    m = mod_ref[0]
    h = (_rms(x_ref[0], n1_ref[...]) * (1.0 + m[1:2]) + m[0:1]).astype(BF16)
    cos = cos_ref[...]
    sin = sin_ref[...]

    z1 = _dot(h, w1_ref[...])
    r0 = MLA_Q_RANK
    r1 = r0 + MLA_KV_RANK
    cq, ckv = z1[:, :r0], z1[:, r0:r1]
    kr, krs = z1[:, r1:r1 + LANES], z1[:, r1 + LANES:]

    qc = _rms(cq, qg_ref[...]).astype(BF16)
    q = _dot(qc, wq_ref[...])
    qs = _dot(qc, wqs_ref[...])
    lat = _rms(ckv, kvg_ref[...])
    lat_ref[0] = lat
    latb = lat.astype(BF16)
    krope = kr * cos + krs * sin
    kr_ref[0] = krope[:, ROPE_LO:ROPE_HI]
    kn = _dot(latb, wk_ref[...])
    v = _dot(latb, wv_ref[...])
    dq = _dot(h, wdq_ref[...]) * (DIFF_SCALE * LOG2E)
    dk = _dot(h, wdk_ref[...])
    dv = _dot(h, wdv_ref[...])
    dkf_ref[0] = dk
    dvf_ref[0] = dv
    for hd in range(HEADS):
        sl = slice(hd * LANES, (hd + 1) * LANES)
        qm_ref[0, hd] = ((q[:, sl] * cos + qs[:, sl] * sin) * (MLA_SCALE * LOG2E)).astype(BF16)
        km_ref[0, hd] = (kn[:, sl] + krope).astype(BF16)
        vm_ref[0, hd] = v[:, sl].astype(BF16)
        qd_ref[0, hd] = dq[:, sl].astype(BF16)
        kd_ref[0, hd] = dk[:, sl].astype(BF16)
        vd_ref[0, hd] = dv[:, sl].astype(BF16)
    g_ref[0] = jax.nn.sigmoid(_dot(h, wg_ref[...])).astype(BF16)


def _proj(x, mod, cos_tab, sin_tab, pos0, tt, w):
    b, s, d = x.shape
    assert s % tt == 0 and pos0 % tt == 0
    off = pos0 // tt
    tok = lambda width: pl.BlockSpec((1, tt, width), lambda bi, i: (bi, i, 0))
    head = pl.BlockSpec((1, HEADS, tt, LANES), lambda bi, i: (bi, 0, i, 0))
    tab = pl.BlockSpec((tt, LANES), lambda bi, i: (off + i, 0))
    hm = jax.ShapeDtypeStruct((b, HEADS, s, LANES), BF16)
    weights = [w["n1"], w["qg"], w["kvg"], w["w1"], w["wq"], w["wqs"], w["wk"], w["wv"],
               w["wdq"], w["wdk"], w["wdv"], w["wg"]]
    return pl.pallas_call(
        _proj_kernel,
        grid=(b, s // tt),
        in_specs=[tok(d), pl.BlockSpec((1, 6, d), lambda bi, i: (bi, 0, 0)), tab, tab]
                 + [_const_spec(a.shape) for a in weights],
        out_specs=[tok(MLA_KV_RANK), tok(MLA_ROPE), tok(HEADS * LANES), tok(HEADS * LANES)]
                  + [head] * 6 + [tok(2 * d)],
        out_shape=[jax.ShapeDtypeStruct((b, s, MLA_KV_RANK), F32),
                   jax.ShapeDtypeStruct((b, s, MLA_ROPE), F32),
                   jax.ShapeDtypeStruct((b, s, HEADS * LANES), F32),
                   jax.ShapeDtypeStruct((b, s, HEADS * LANES), F32)]
                  + [hm] * 6 + [jax.ShapeDtypeStruct((b, s, 2 * d), BF16)],
        compiler_params=_params(("parallel", "parallel")),
        name="proj",
    )(x, mod, cos_tab, sin_tab, *weights)


def _online_softmax(s, v, m_ref, l_ref, acc_ref, h):
    m_prev = m_ref[h]
    m_new = jnp.maximum(m_prev, jnp.max(s, axis=1, keepdims=True))
    alpha = jnp.exp2(m_prev - m_new)
    p = jnp.exp2(s - jnp.tile(m_new, (1, s.shape[1] // LANES)))
    l_ref[h] = alpha * l_ref[h] + jnp.sum(p, axis=1, keepdims=True)
    acc_ref[h] = alpha * acc_ref[h] + _dot(p.astype(BF16), v)
    m_ref[h] = m_new


def _diff_queries(qd):
    lane = lax.broadcasted_iota(jnp.int32, (1, LANES), 1)
    first = (lane < DIFF_HD).astype(qd.dtype)
    return jnp.concatenate([qd * first, qd * (1 - first)], axis=0)


def _diff_output(o12, lam, sg, post_scale):
    rows = o12.shape[0] // 2
    o = o12[:rows] - lam * o12[rows:]
    return _rms(o, sg) * post_scale


def _attn_kernel(qi_ref, kj_ref, cls_ref, tab_ref, qm_ref, qd_ref, km_ref, vm_ref, kd_ref, vd_ref, lam_ref, sg_ref,
                 om_ref, od_ref,
                 bias_s, mask_s, m_m, l_m, acc_m, m_d, l_d, acc_d, *, post_scale):
    tq = qm_ref.shape[2]
    tk = km_ref.shape[2]
    n_diag = tk // tq
    p = pl.program_id(1)
    kj = kj_ref[p]
    cls = cls_ref[p]

    @pl.when((pl.program_id(0) == 0) & (p == 0))
    def _build_bias():
        row = lax.broadcasted_iota(jnp.int32, (tq, tk), 0)
        col = lax.broadcasted_iota(jnp.int32, (tq, tk), 1)
        for near in range(n_diag + 1):
            qrow = row + (near * tq if near < n_diag else tk)
            bucket = _rel_bucket(col - qrow)
            masked = (col >> CHUNK_SHIFT) > (qrow >> CHUNK_SHIFT)
            if near < n_diag:
                mask_s[near] = jnp.where(masked, NEG, 0.0)

            def per_head(h, carry):
                far = tab_ref[REL_HALF - 1, h]
                b = (_bias_from_table(bucket, tab_ref, h) - far) * LOG2E
                bias_s[near, h] = jnp.where(masked, NEG, b) if near < n_diag else b
                return carry

            lax.fori_loop(0, HEADS, per_head, 0)

    @pl.when(kj == 0)
    def _init():
        m_m[...] = jnp.full(m_m.shape, NEG, F32)
        m_d[...] = jnp.full(m_d.shape, NEG, F32)
        l_m[...] = jnp.zeros(l_m.shape, F32)
        l_d[...] = jnp.zeros(l_d.shape, F32)
        acc_m[...] = jnp.zeros(acc_m.shape, F32)
        acc_d[...] = jnp.zeros(acc_d.shape, F32)

    def step(near):
        keys = (near + 1) * tq if near is not None and near < n_diag else tk
        for h in range(HEADS):
            s = _dot_nt(qm_ref[0, h], km_ref[0, h, :keys, :])
            if near is not None and near < n_diag:
                s = s + mask_s[near, :, :keys]
            _online_softmax(s, vm_ref[0, h, :keys, :], m_m, l_m, acc_m, h)
            s12 = _dot_nt(_diff_queries(qd_ref[0, h]), kd_ref[0, h, :keys, :])
            if near is not None:
                b = bias_s[near, h, :, :keys]
                s12 = s12 + jnp.concatenate([b, b], axis=0)
            _online_softmax(s12, vd_ref[0, h, :keys, :], m_d, l_d, acc_d, h)

    @pl.when(cls == n_diag + 1)
    def _far():
        step(None)

    @pl.when(cls == n_diag)
    def _before_diagonal():
        step(n_diag)

    for r in range(n_diag):
        @pl.when(cls == r)
        def _diagonal():
            step(r)
            lam = lam_ref[0:1, :]
            sg = sg_ref[...]

            def finish(h, carry):
                om_ref[0, h] = (acc_m[h] / l_m[h]).astype(om_ref.dtype)
                od_ref[0, h] = _diff_output(acc_d[h] / l_d[h], lam, sg, post_scale).astype(od_ref.dtype)
                return carry

            lax.fori_loop(0, HEADS, finish, 0)


def _attn_prompt(rel_table, qm, qd, km, vm, kd, vd, lamv, sg, tq, tk, post_scale):
    b, _, s, _ = qm.shape
    assert s % tk == 0 and tk % tq == 0 and tq % CHUNK == 0 and tq % LANES == 0 and tq + 1 >= REL_FAR
    n_diag = tk // tq
    qi, kj, cls = [], [], []
    for i in range(s // tq):
        jd, r = divmod(i, n_diag)
        for j in range(jd + 1):
            qi.append(i)
            kj.append(j)
            cls.append(r if j == jd else n_diag if (j == jd - 1 and r == 0) else n_diag + 1)
    tables = [jnp.asarray(np.array(a, np.int32)) for a in (qi, kj, cls)]
    qspec = pl.BlockSpec((1, HEADS, tq, LANES), lambda bi, p, qi_r, kj_r, c_r: (bi, 0, qi_r[p], 0))
    kspec = pl.BlockSpec((1, HEADS, tk, LANES), lambda bi, p, qi_r, kj_r, c_r: (bi, 0, kj_r[p], 0))
    stat = lambda rows: pltpu.VMEM((HEADS, rows, LANES), F32)
    out = jax.ShapeDtypeStruct((b, HEADS, s, LANES), BF16)
    return pl.pallas_call(
        functools.partial(_attn_kernel, post_scale=post_scale),
        grid_spec=pltpu.PrefetchScalarGridSpec(
            num_scalar_prefetch=3,
            grid=(b, len(qi)),
            in_specs=[pl.BlockSpec(memory_space=pltpu.SMEM),
                      qspec, qspec, kspec, kspec, kspec, kspec,
                      pl.BlockSpec((8, LANES), lambda bi, p, qi_r, kj_r, c_r: (0, 0)),
                      pl.BlockSpec((1, LANES), lambda bi, p, qi_r, kj_r, c_r: (0, 0))],
            out_specs=[qspec, qspec],
            scratch_shapes=[pltpu.VMEM((n_diag + 1, HEADS, tq, tk), F32), pltpu.VMEM((n_diag, tq, tk), F32),
                            stat(tq), stat(tq), stat(tq), stat(2 * tq), stat(2 * tq), stat(2 * tq)],
        ),
        out_shape=[out, out],
        compiler_params=_params(("arbitrary", "arbitrary")),
        name="attn_prompt",
    )(*tables, rel_table, qm, qd, km, vm, kd, vd, lamv, sg)


def _attn_sample_kernel(tab_ref, qm_ref, qd_ref, kmn_ref, vmn_ref, kdn_ref, vdn_ref,
                        latc_ref, krc_ref, kdc_ref, vdc_ref, wk_ref, wv_ref, lam_ref, sg_ref,
                        om_ref, od_ref, *, post_scale):
    h = pl.program_id(1)
    nq = qm_ref.shape[2]
    past = latc_ref.shape[1]

    latb = latc_ref[0].astype(BF16)
    r = lax.broadcasted_iota(jnp.int32, (MLA_ROPE, LANES), 0)
    c = lax.broadcasted_iota(jnp.int32, (MLA_ROPE, LANES), 1)
    place = (c == r + ROPE_LO).astype(BF16)
    kc = (_dot(latb, wk_ref[...]) + _dot(krc_ref[0].astype(BF16), place)).astype(BF16)
    vc = _dot(latb, wv_ref[...]).astype(BF16)

    def geometry(reps, n_keys, key0):
        row = lax.broadcasted_iota(jnp.int32, (reps * nq, n_keys), 0)
        qpos = past + jnp.where(row >= nq, row - nq, row)
        kpos = key0 + lax.broadcasted_iota(jnp.int32, (reps * nq, n_keys), 1)
        visible = (kpos >> CHUNK_SHIFT) <= (qpos >> CHUNK_SHIFT)
        return visible, kpos - qpos

    def softmax_pv(s_c, s_n, v_c, v_n):
        m = jnp.maximum(jnp.max(s_c, axis=1, keepdims=True), jnp.max(s_n, axis=1, keepdims=True))
        p_c = jnp.exp2(s_c - m)
        p_n = jnp.exp2(s_n - m)
        l = jnp.sum(p_c, axis=1, keepdims=True) + jnp.sum(p_n, axis=1, keepdims=True)
        return (_dot(p_c.astype(BF16), v_c) + _dot(p_n.astype(BF16), v_n)) / l

    q = qm_ref[0, 0]
    vis_c, _ = geometry(1, past, 0)
    vis_n, _ = geometry(1, nq, past)
    s_c = jnp.where(vis_c, _dot_nt(q, kc), NEG)
    s_n = jnp.where(vis_n, _dot_nt(q, kmn_ref[0, 0]), NEG)
    om_ref[0, 0] = softmax_pv(s_c, s_n, vc, vmn_ref[0, 0]).astype(om_ref.dtype)

    q12 = _diff_queries(qd_ref[0, 0])
    vis_c, rel_c = geometry(2, past, 0)
    vis_n, rel_n = geometry(2, nq, past)
    b_c = _bias_from_table(_rel_bucket(rel_c), tab_ref, h) * LOG2E
    b_n = _bias_from_table(_rel_bucket(rel_n), tab_ref, h) * LOG2E
    s_c = jnp.where(vis_c, _dot_nt(q12, kdc_ref[0].astype(BF16)) + b_c, NEG)
    s_n = jnp.where(vis_n, _dot_nt(q12, kdn_ref[0, 0]) + b_n, NEG)
    o12 = softmax_pv(s_c, s_n, vdc_ref[0].astype(BF16), vdn_ref[0, 0])
    od_ref[0, 0] = _diff_output(o12, lam_ref[0:1, :], sg_ref[...], post_scale).astype(od_ref.dtype)


def _attn_sample(rel_table, qm, qd, kmn, vmn, kdn, vdn, latc, krc, kdc, vdc, wk, wv, lamv, sg, post_scale):
    b, _, nq, _ = qm.shape
    past = latc.shape[1]
    new = pl.BlockSpec((1, 1, nq, LANES), lambda bi, h: (bi, h, 0, 0))
    cache_head = pl.BlockSpec((1, past, LANES), lambda bi, h: (bi, 0, h))
    out = jax.ShapeDtypeStruct((b, HEADS, nq, LANES), BF16)
    return pl.pallas_call(
        functools.partial(_attn_sample_kernel, post_scale=post_scale),
        grid=(b, HEADS),
        in_specs=[pl.BlockSpec(memory_space=pltpu.SMEM), new, new, new, new, new, new,
                  pl.BlockSpec((1, past, MLA_KV_RANK), lambda bi, h: (bi, 0, 0)),
                  pl.BlockSpec((1, past, MLA_ROPE), lambda bi, h: (bi, 0, 0)),
                  cache_head, cache_head,
                  pl.BlockSpec((MLA_KV_RANK, LANES), lambda bi, h: (0, h)),
                  pl.BlockSpec((MLA_KV_RANK, LANES), lambda bi, h: (0, h)),
                  pl.BlockSpec((8, LANES), lambda bi, h: (0, 0)),
                  pl.BlockSpec((1, LANES), lambda bi, h: (0, 0))],
        out_specs=[new, new],
        out_shape=[out, out],
        compiler_params=_params(("parallel", "parallel")),
        name="attn_sample",
    )(rel_table, qm, qd, kmn, vmn, kdn, vdn, latc, krc, kdc, vdc, wk, wv, lamv, sg)


def _post_kernel(x_ref, om_ref, od_ref, g_ref, mod_ref, wom_ref, wod_ref, wout_ref, n2_ref,
                 wrh_ref, wrl_ref, br_ref, x1_ref, h2_ref, gate_ref):
    d = x_ref.shape[2]
    om = jnp.concatenate([om_ref[0, hd] for hd in range(HEADS)], axis=1)
    od = jnp.concatenate([od_ref[0, hd] for hd in range(HEADS)], axis=1)
    g = g_ref[0].astype(F32)
    merged = g[:, :d] * _dot(om, wom_ref[...]) + g[:, d:] * _dot(od, wod_ref[...])
    m = mod_ref[0]
    x1 = x_ref[0] + m[2:3] * _dot(merged.astype(BF16), wout_ref[...])
    x1_ref[0] = x1
    h2 = _rms(x1, n2_ref[...]) * (1.0 + m[4:5]) + m[3:4]
    h2_ref[0] = h2.astype(BF16)

    hi, lo = _split_bf16(h2)
    wrh = wrh_ref[...]
    logits = _dot(hi, wrh) + (_dot(hi, wrl_ref[...]) + _dot(lo, wrh)) + br_ref[...]
    lane = lax.broadcasted_iota(jnp.int32, logits.shape, 1).astype(F32)
    work = logits
    tops, picks = [], []
    for _ in range(TOP_K):
        top = jnp.max(work, axis=1, keepdims=True)
        first = jnp.min(jnp.where(work == top, lane, float(N_EXPERTS)), axis=1, keepdims=True)
        pick = lane == first
        tops.append(top)
        picks.append(pick)
        work = jnp.where(pick, -jnp.inf, work)
    es = [jnp.exp(top - tops[0]) for top in tops]
    den = es[0]
    for e in es[1:]:
        den = den + e
    gate = jnp.zeros(logits.shape, F32)
    for e, pick in zip(es, picks):
        gate = jnp.where(pick, e / den, gate)
    gate_ref[0] = gate


def _post(x, om, od, g, mod, tt, w):
    b, s, d = x.shape
    tok = lambda width: pl.BlockSpec((1, tt, width), lambda bi, i: (bi, i, 0))
    head = pl.BlockSpec((1, HEADS, tt, LANES), lambda bi, i: (bi, 0, i, 0))
    weights = [w["wom"], w["wod"], w["wout"], w["n2"], w["wrh"], w["wrl"], w["br"]]
    return pl.pallas_call(
        _post_kernel,
        grid=(b, s // tt),
        in_specs=[tok(d), head, head, tok(2 * d), pl.BlockSpec((1, 6, d), lambda bi, i: (bi, 0, 0))]
                 + [_const_spec(a.shape) for a in weights],
        out_specs=[tok(d), tok(d), tok(N_EXPERTS)],
        out_shape=[jax.ShapeDtypeStruct((b, s, d), F32),
                   jax.ShapeDtypeStruct((b, s, d), BF16),
                   jax.ShapeDtypeStruct((b, s, N_EXPERTS), F32)],
        compiler_params=_params(("parallel", "parallel")),
        name="post",
    )(x, om, od, g, mod, *weights)


def _moe_kernel(h_ref, gate_ref, win_ref, bin_ref, wdn_ref, bdn_ref, f_ref, acc_ref, *, f_chunk):
    e = pl.program_id(1)
    gate = gate_ref[...]
    d_exp = wdn_ref.shape[1]

    @pl.when(e == 0)
    def _():
        hi, lo = _split_bf16(gate)
        bdn = bdn_ref[...]
        acc_ref[...] = _dot(hi, bdn) + _dot(lo, bdn)

    lane = lax.broadcasted_iota(jnp.int32, gate.shape, 1)
    ge = jnp.sum(jnp.where(lane == e, gate, 0.0), axis=1, keepdims=True)
    h = h_ref[...]
    for c0 in range(0, d_exp, f_chunk):
        c1 = c0 + f_chunk
        gl = _dot(h, win_ref[0, :, c0:c1]) + bin_ref[0, :, c0:c1]
        up = _dot(h, win_ref[0, :, d_exp + c0:d_exp + c1]) + bin_ref[0, :, d_exp + c0:d_exp + c1]
        gl = jnp.minimum(gl, SWIGLU_LIMIT)
        up = jnp.clip(up, -SWIGLU_LIMIT, SWIGLU_LIMIT)
        act = gl * jax.nn.sigmoid(SWIGLU_ALPHA * gl) * (up + 1.0) * ge
        acc_ref[...] += _dot(act.astype(BF16), wdn_ref[0, c0:c1, :])

    @pl.when(e == pl.num_programs(1) - 1)
    def _():
        f_ref[...] = acc_ref[...]


def _moe(h2, gate, w, tm):
    n, d = h2.shape
    n_exp, _, two_f = w["win"].shape
    d_exp = two_f // 2
    assert n % tm == 0
    return pl.pallas_call(
        functools.partial(_moe_kernel, f_chunk=min(512, d_exp)),
        grid=(n // tm, n_exp),
        in_specs=[pl.BlockSpec((tm, d), lambda i, e: (i, 0)),
                  pl.BlockSpec((tm, n_exp), lambda i, e: (i, 0)),
                  pl.BlockSpec((1, d, two_f), lambda i, e: (e, 0, 0)),
                  pl.BlockSpec((1, 1, two_f), lambda i, e: (e, 0, 0)),
                  pl.BlockSpec((1, d_exp, d), lambda i, e: (e, 0, 0)),
                  pl.BlockSpec((n_exp, d), lambda i, e: (0, 0))],
        out_specs=pl.BlockSpec((tm, d), lambda i, e: (i, 0)),
        out_shape=jax.ShapeDtypeStruct((n, d), F32),
        scratch_shapes=[pltpu.VMEM((tm, d), F32)],
        compiler_params=_params(("parallel", "arbitrary")),
        name="moe",
    )(h2, gate, w["win"], w["bin"], w["wdn"], w["bdn"])


def _final_kernel(x1_ref, f_ref, mod_ref, g_ref, y_ref, *, normalize):
    y = x1_ref[0] + mod_ref[0][5:6] * f_ref[0]
    y_ref[0] = _rms(y, g_ref[...]) if normalize else y


def _final(x1, f, mod, g, tt, normalize):
    b, s, d = x1.shape
    tok = pl.BlockSpec((1, tt, d), lambda bi, i: (bi, i, 0))
    return pl.pallas_call(
        functools.partial(_final_kernel, normalize=normalize),
        grid=(b, s // tt),
        in_specs=[tok, tok, pl.BlockSpec((1, 6, d), lambda bi, i: (bi, 0, 0)),
                  pl.BlockSpec((1, d), lambda bi, i: (0, 0))],
        out_specs=tok,
        out_shape=jax.ShapeDtypeStruct((b, s, d), F32),
        compiler_params=_params(("parallel", "parallel")),
        name="final",
    )(x1, f, mod, g)


def _layer_weights(l, w_in, q_norm_g, w_uq, kv_norm_g, w_ukv, norm1_g, subln_g, w_o_mla, w_o_diff, w_out,
                   norm2_g, w_router, b_router, w_e_in, b_e_in, w_e_down, b_e_down):
    d = w_in.shape[1]
    hw = HEADS * LANES
    sizes = (MLA_Q_RANK, MLA_KV_RANK, MLA_ROPE, hw, hw, hw, 2 * d)
    offs = np.concatenate([[0], np.cumsum(sizes)])
    pieces = [w_in[l][:, offs[i]:offs[i + 1]] for i in range(len(sizes))]
    w_cq, w_ckv, w_kr, w_dq, w_dk, w_dv, w_g = pieces
    half = MLA_ROPE // 2
    pad_lo = jnp.zeros((d, ROPE_LO), F32)
    pad_hi = jnp.zeros((d, LANES - ROPE_HI), F32)
    kr_lanes = jnp.concatenate([pad_lo, w_kr, pad_hi], axis=1)
    kr_swap = jnp.concatenate([pad_lo, w_kr[:, half:], w_kr[:, :half], pad_hi], axis=1)
    w1 = jnp.concatenate([w_cq, w_ckv, kr_lanes, kr_swap], axis=1)

    uq = w_uq[l].reshape(MLA_Q_RANK, HEADS, MLA_NOPE + MLA_ROPE)
    nope, rope = uq[..., :MLA_NOPE], uq[..., MLA_NOPE:]
    zq = lambda n: jnp.zeros((MLA_Q_RANK, HEADS, n), F32)
    wq = jnp.concatenate([nope, rope, zq(LANES - ROPE_HI)], axis=-1).reshape(MLA_Q_RANK, hw)
    wqs = jnp.concatenate([zq(ROPE_LO), rope[..., half:], rope[..., :half], zq(LANES - ROPE_HI)],
                          axis=-1).reshape(MLA_Q_RANK, hw)
    ukv = w_ukv[l].reshape(MLA_KV_RANK, HEADS, MLA_NOPE + MLA_V)
    zk = lambda n: jnp.zeros((MLA_KV_RANK, HEADS, n), F32)
    wk = jnp.concatenate([ukv[..., :MLA_NOPE], zk(LANES - MLA_NOPE)], axis=-1).reshape(MLA_KV_RANK, hw)
    wv = jnp.concatenate([ukv[..., MLA_NOPE:], zk(LANES - MLA_V)], axis=-1).reshape(MLA_KV_RANK, hw)
    wom = jnp.concatenate([w_o_mla[l].reshape(HEADS, MLA_V, d), jnp.zeros((HEADS, LANES - MLA_V, d), F32)],
                          axis=1).reshape(hw, d)
    wrh, wrl = _split_bf16(w_router[l])
    bf = lambda a: a.astype(BF16)
    row = lambda a: a.reshape(1, -1)
    return dict(
        n1=row(norm1_g[l]), qg=row(q_norm_g[l]), kvg=row(kv_norm_g[l]),
        w1=bf(w1), wq=bf(wq), wqs=bf(wqs), wk=bf(wk), wv=bf(wv),
        wdq=bf(w_dq), wdk=bf(w_dk), wdv=bf(w_dv), wg=bf(w_g),
        sg=row(subln_g[l]), wom=bf(wom), wod=bf(w_o_diff[l]), wout=bf(w_out[l]),
        n2=row(norm2_g[l]), wrh=wrh, wrl=wrl, br=row(b_router[l]),
        win=bf(w_e_in[l]), bin=b_e_in[l][:, None, :], wdn=bf(w_e_down[l]), bdn=bf(b_e_down[l]),
    )


def _tile(n, target):
    t = min(n, target)
    while n % t:
        t //= 2
    return t


def kernel(x_prompt, x_sample, cache_mla_latent, cache_mla_krope, cache_diff_k, cache_diff_v,
           c_prompt, c_sample, rel_table, final_norm_g, w_ada, b_ada, norm1_g, w_in, q_norm_g, w_uq,
           kv_norm_g, w_ukv, lam_q1, lam_k1, lam_q2, lam_k2, subln_g, w_o_mla, w_o_diff, w_out,
           norm2_g, w_router, b_router, w_e_in, b_e_in, w_e_down, b_e_down):
    depth = w_ada.shape[0]
    bp, sp, d = x_prompt.shape
    bs, ss, _ = x_sample.shape
    past = cache_mla_latent.shape[2]
    assert HEADS * DIFF_VD == HEADS * LANES and past % ss == 0 and ss % SUBLANES_BF16 == 0

    n_pos = max(sp, past + ss)
    rope_rows = _tile(n_pos, 2048)
    cos_tab, sin_tab = _rope_tables(-(-n_pos // rope_rows) * rope_rows, rope_rows)

    pad_rows = -(bp + bs) % SUBLANES_BF16
    c_all = jnp.concatenate([c_prompt, c_sample, jnp.zeros((pad_rows, d), F32)], axis=0)

    g_fin = final_norm_g.reshape(1, d)
    hp, hs = x_prompt, x_sample
    rows_p, rows_s = [], []
    for l in range(depth):
        lam_init = 0.8 - 0.6 * math.exp(-0.3 * l)
        w = _layer_weights(l, w_in, q_norm_g, w_uq, kv_norm_g, w_ukv, norm1_g, subln_g, w_o_mla, w_o_diff,
                           w_out, norm2_g, w_router, b_router, w_e_in, b_e_in, w_e_down, b_e_down)
        lam_vecs = jnp.stack([lam_q1[l], lam_k1[l], lam_q2[l], lam_k2[l]])
        mod, lamv = _ada(c_all, w_ada[l].astype(BF16), b_ada[l].reshape(1, -1), lam_vecs, lam_init)
        mod_p = mod[:bp].reshape(bp, 6, d)
        mod_s = mod[bp:bp + bs].reshape(bs, 6, d)
        post_scale = 1.0 - lam_init

        tt = _tile(sp, 256)
        lat, kr, dkf, dvf, qm, km, vm, qd, kd, vd, g = _proj(hp, mod_p, cos_tab, sin_tab, 0, tt, w)
        om, od = _attn_prompt(rel_table, qm, qd, km, vm, kd, vd, lamv, w["sg"], _tile(sp, 256), _tile(sp, 256), post_scale)
        x1, h2, gate = _post(hp, om, od, g, mod_p, tt, w)
        f = _moe(h2.reshape(bp * sp, d), gate.reshape(bp * sp, N_EXPERTS), w, _tile(bp * sp, 1024))
        hp = (x1, f.reshape(bp, sp, d), mod_p)
        rows_p.append((lat, kr, dkf.reshape(bp, sp, HEADS, 2 * DIFF_HD), dvf.reshape(bp, sp, HEADS, DIFF_VD)))

        lat, kr, dkf, dvf, qm, km, vm, qd, kd, vd, g = _proj(hs, mod_s, cos_tab, sin_tab, past, ss, w)
        om, od = _attn_sample(rel_table, qm, qd, km, vm, kd, vd,
                              cache_mla_latent[l], cache_mla_krope[l],
                              cache_diff_k[l].reshape(bs, past, HEADS * 2 * DIFF_HD),
                              cache_diff_v[l].reshape(bs, past, HEADS * DIFF_VD),
                              w["wk"], w["wv"], lamv, w["sg"], post_scale)
        x1, h2, gate = _post(hs, om, od, g, mod_s, ss, w)
        f = _moe(h2.reshape(bs * ss, d), gate.reshape(bs * ss, N_EXPERTS), w, bs * ss)
        hs = (x1, f.reshape(bs, ss, d), mod_s)
        rows_s.append((lat, kr, dkf.reshape(bs, ss, HEADS, 2 * DIFF_HD), dvf.reshape(bs, ss, HEADS, DIFF_VD)))

        if l + 1 < depth:
            hp = _final(*hp, g_fin, _tile(sp, 1024), normalize=False)
            hs = _final(*hs, g_fin, ss, normalize=False)

    y_prompt = _final(*hp, g_fin, _tile(sp, 1024), normalize=True)
    y_sample = _final(*hs, g_fin, ss, normalize=True)
    stack = lambda rows, i: jnp.stack([r[i] for r in rows])
    return (y_prompt, y_sample,
            stack(rows_p, 0), stack(rows_p, 1), stack(rows_p, 2), stack(rows_p, 3),
            stack(rows_s, 0), stack(rows_s, 1), stack(rows_s, 2), stack(rows_s, 3))
```

```python
import functools
import math

import jax
import jax.numpy as jnp
import numpy as np
from jax import lax
from jax.experimental import pallas as pl
from jax.experimental.pallas import tpu as pltpu

F32 = jnp.float32
BF16 = jnp.bfloat16

CHUNK = 64
EPS = 1e-6
HEADS = 8
MLA_Q_RANK = 384
MLA_KV_RANK = 256
MLA_NOPE = 64
MLA_ROPE = 32
MLA_V = 64
MLA_SCALE = (MLA_NOPE + MLA_ROPE) ** -0.5
ROPE_THETA = 10000.0
DIFF_HD = 64
DIFF_VD = 2 * DIFF_HD
DIFF_SCALE = DIFF_HD ** -0.5
REL_BUCKETS = 32
REL_MAX_DIST = 128
N_EXPERTS = 32
TOP_K = 4
SWIGLU_LIMIT = 7.0
SWIGLU_ALPHA = 1.702
NEG = -1e30
LOG2E = math.log2(math.e)

LANES = 128
SUBLANES_BF16 = 16
VMEM_LIMIT = 56 * 1024 * 1024

ROPE_LO = MLA_NOPE
ROPE_HI = MLA_NOPE + MLA_ROPE
CHUNK_SHIFT = CHUNK.bit_length() - 1
assert 1 << CHUNK_SHIFT == CHUNK

REL_HALF = REL_BUCKETS // 2
REL_EXACT = REL_HALF // 2
assert (REL_MAX_DIST // REL_EXACT) == 1 << (REL_HALF - REL_EXACT) // 2
REL_FAR = math.isqrt(REL_EXACT * REL_EXACT * 2 ** (REL_HALF - REL_EXACT - 1) - 1) + 1


def _params(sem, vmem=VMEM_LIMIT):
    return pltpu.CompilerParams(dimension_semantics=sem, vmem_limit_bytes=vmem)


def _const_spec(shape):
    zeros = (0,) * len(shape)
    return pl.BlockSpec(shape, lambda *_: zeros, pipeline_mode=pl.Buffered(1))


def _dot(a, b):
    return jnp.dot(a, b, preferred_element_type=F32)


def _dot_nt(a, b):
    return lax.dot_general(a, b, (((1,), (1,)), ((), ())), preferred_element_type=F32)


def _rms(x, g):
    return x * lax.rsqrt(jnp.mean(x * x, axis=-1, keepdims=True) + EPS) * g


def _split_bf16(x):
    hi = x.astype(BF16)
    lo = (x - hi.astype(F32)).astype(BF16)
    return hi, lo


def _rel_bucket(rel):
    n = jnp.abs(rel)
    n2 = n * n
    cnt = jnp.zeros_like(n)
    for j in range(1, REL_HALF - REL_EXACT):
        cnt = cnt + (n2 >= REL_EXACT * REL_EXACT * 2 ** j).astype(jnp.int32)
    large = REL_EXACT + cnt
    return jnp.where(rel > 0, REL_HALF, 0) + jnp.where(n < REL_EXACT, n, large)


def _bias_from_table(bucket, tab_ref, h):
    out = jnp.zeros(bucket.shape, F32)
    for b in range(REL_BUCKETS):
        out = jnp.where(bucket == b, tab_ref[b, h], out)
    return out


def _ada_kernel(c_ref, w_ref, b_ref, lamv_ref, mod_ref, lam_ref, *, lam_init):
    c = c_ref[...]
    s = (c * jax.nn.sigmoid(c)).astype(BF16)
    mod_ref[...] = _dot(s, w_ref[...]) + b_ref[...]

    @pl.when(pl.program_id(0) == 0)
    def _():
        v = lamv_ref[...]
        a = jnp.sum(v[0:1] * v[1:2], axis=1, keepdims=True)
        b = jnp.sum(v[2:3] * v[3:4], axis=1, keepdims=True)
        lam_ref[...] = jnp.broadcast_to(jnp.exp(a) - jnp.exp(b) + lam_init, lam_ref.shape)


def _ada(c_all, w_ada, b_ada, lam_vecs, lam_init):
    rows, d = c_all.shape
    n = w_ada.shape[1]
    tn = d
    return pl.pallas_call(
        functools.partial(_ada_kernel, lam_init=lam_init),
        grid=(n // tn,),
        in_specs=[
            pl.BlockSpec((rows, d), lambda j: (0, 0)),
            pl.BlockSpec((d, tn), lambda j: (0, j)),
            pl.BlockSpec((1, tn), lambda j: (0, j)),
            pl.BlockSpec(lam_vecs.shape, lambda j: (0, 0)),
        ],
        out_specs=[
            pl.BlockSpec((rows, tn), lambda j: (0, j)),
            pl.BlockSpec((8, LANES), lambda j: (0, 0)),
        ],
        out_shape=[
            jax.ShapeDtypeStruct((rows, n), F32),
            jax.ShapeDtypeStruct((8, LANES), F32),
        ],
        compiler_params=_params(("arbitrary",)),
        name="ada",
    )(c_all, w_ada, b_ada, lam_vecs)


def _rope_kernel(inv_ref, cos_ref, sin_ref):
    rows = cos_ref.shape[0]
    pos = pl.program_id(0) * rows + lax.broadcasted_iota(jnp.int32, (rows, LANES), 0)
    lane = lax.broadcasted_iota(jnp.int32, (rows, LANES), 1)
    ang = pos.astype(F32) * inv_ref[...]
    in_rope = (lane >= ROPE_LO) & (lane < ROPE_HI)
    first_half = lane < ROPE_LO + MLA_ROPE // 2
    cos_ref[...] = jnp.where(lane < ROPE_LO, 1.0, jnp.where(in_rope, jnp.cos(ang), 0.0))
    sn = jnp.sin(ang)
    sin_ref[...] = jnp.where(in_rope, jnp.where(first_half, -sn, sn), 0.0)


def _rope_tables(n_pos, rows):
    half = MLA_ROPE // 2
    inv = ROPE_THETA ** (-jnp.arange(half, dtype=F32) / half)
    inv_lanes = jnp.zeros((1, LANES), F32).at[0, ROPE_LO:ROPE_HI].set(jnp.concatenate([inv, inv]))
    return pl.pallas_call(
        _rope_kernel,
        grid=(n_pos // rows,),
        in_specs=[pl.BlockSpec((1, LANES), lambda i: (0, 0))],
        out_specs=[pl.BlockSpec((rows, LANES), lambda i: (i, 0))] * 2,
        out_shape=[jax.ShapeDtypeStruct((n_pos, LANES), F32)] * 2,
        compiler_params=_params(("parallel",)),
        name="rope_tables",
    )(inv_lanes)


def _proj_kernel(x_ref, mod_ref, cos_ref, sin_ref, n1_ref, qg_ref, kvg_ref,
                 w1_ref, wq_ref, wqs_ref, wk_ref, wv_ref, wdq_ref, wdk_ref, wdv_ref, wg_ref,
                 lat_ref, kr_ref, dkf_ref, dvf_ref,
                 qm_ref, km_ref, vm_ref, qd_ref, kd_ref, vd_ref, g_ref):
    m = mod_ref[0]
    h = (_rms(x_ref[0], n1_ref[...]) * (1.0 + m[1:2]) + m[0:1]).astype(BF16)
    cos = cos_ref[...]
    sin = sin_ref[...]

    z1 = _dot(h, w1_ref[...])
    r0 = MLA_Q_RANK
    r1 = r0 + MLA_KV_RANK
    cq, ckv = z1[:, :r0], z1[:, r0:r1]
    kr, krs = z1[:, r1:r1 + LANES], z1[:, r1 + LANES:]

    qc = _rms(cq, qg_ref[...]).astype(BF16)
    q = _dot(qc, wq_ref[...])
    qs = _dot(qc, wqs_ref[...])
    lat = _rms(ckv, kvg_ref[...])
    lat_ref[0] = lat
    latb = lat.astype(BF16)
    krope = kr * cos + krs * sin
    kr_ref[0] = krope[:, ROPE_LO:ROPE_HI]
    kn = _dot(latb, wk_ref[...])
    v = _dot(latb, wv_ref[...])
    dq = _dot(h, wdq_ref[...]) * (DIFF_SCALE * LOG2E)
    dk = _dot(h, wdk_ref[...])
    dv = _dot(h, wdv_ref[...])
    dkf_ref[0] = dk
    dvf_ref[0] = dv
    for hd in range(HEADS):
        sl = slice(hd * LANES, (hd + 1) * LANES)
        qm_ref[0, hd] = ((q[:, sl] * cos + qs[:, sl] * sin) * (MLA_SCALE * LOG2E)).astype(BF16)
        km_ref[0, hd] = (kn[:, sl] + krope).astype(BF16)
        vm_ref[0, hd] = v[:, sl].astype(BF16)
        qd_ref[0, hd] = dq[:, sl].astype(BF16)
        kd_ref[0, hd] = dk[:, sl].astype(BF16)
        vd_ref[0, hd] = dv[:, sl].astype(BF16)
    g_ref[0] = jax.nn.sigmoid(_dot(h, wg_ref[...])).astype(BF16)


def _proj(x, mod, cos_tab, sin_tab, pos0, tt, w):
    b, s, d = x.shape
    assert s % tt == 0 and pos0 % tt == 0
    off = pos0 // tt
    tok = lambda width: pl.BlockSpec((1, tt, width), lambda bi, i: (bi, i, 0))
    head = pl.BlockSpec((1, HEADS, tt, LANES), lambda bi, i: (bi, 0, i, 0))
    tab = pl.BlockSpec((tt, LANES), lambda bi, i: (off + i, 0))
    hm = jax.ShapeDtypeStruct((b, HEADS, s, LANES), BF16)
    weights = [w["n1"], w["qg"], w["kvg"], w["w1"], w["wq"], w["wqs"], w["wk"], w["wv"],
               w["wdq"], w["wdk"], w["wdv"], w["wg"]]
    return pl.pallas_call(
        _proj_kernel,
        grid=(b, s // tt),
        in_specs=[tok(d), pl.BlockSpec((1, 6, d), lambda bi, i: (bi, 0, 0)), tab, tab]
                 + [_const_spec(a.shape) for a in weights],
        out_specs=[tok(MLA_KV_RANK), tok(MLA_ROPE), tok(HEADS * LANES), tok(HEADS * LANES)]
                  + [head] * 6 + [tok(2 * d)],
        out_shape=[jax.ShapeDtypeStruct((b, s, MLA_KV_RANK), F32),
                   jax.ShapeDtypeStruct((b, s, MLA_ROPE), F32),
                   jax.ShapeDtypeStruct((b, s, HEADS * LANES), F32),
                   jax.ShapeDtypeStruct((b, s, HEADS * LANES), F32)]
                  + [hm] * 6 + [jax.ShapeDtypeStruct((b, s, 2 * d), BF16)],
        compiler_params=_params(("parallel", "parallel")),
        name="proj",
    )(x, mod, cos_tab, sin_tab, *weights)


def _online_softmax(s, v, m_ref, l_ref, acc_ref, h):
    m_prev = m_ref[h]
    m_new = jnp.maximum(m_prev, jnp.max(s, axis=1, keepdims=True))
    alpha = jnp.exp2(m_prev - m_new)
    p = jnp.exp2(s - jnp.tile(m_new, (1, s.shape[1] // LANES)))
    l_ref[h] = alpha * l_ref[h] + jnp.sum(p, axis=1, keepdims=True)
    acc_ref[h] = alpha * acc_ref[h] + _dot(p.astype(BF16), v)
    m_ref[h] = m_new


def _diff_queries(qd):
    lane = lax.broadcasted_iota(jnp.int32, (1, LANES), 1)
    first = (lane < DIFF_HD).astype(qd.dtype)
    return jnp.concatenate([qd * first, qd * (1 - first)], axis=0)


def _diff_output(o12, lam, sg, post_scale):
    rows = o12.shape[0] // 2
    o = o12[:rows] - lam * o12[rows:]
    return _rms(o, sg) * post_scale


def _attn_kernel(qi_ref, kj_ref, cls_ref, tab_ref, qm_ref, qd_ref, km_ref, vm_ref, kd_ref, vd_ref, lam_ref, sg_ref,
                 om_ref, od_ref,
                 bias_s, mask_s, m_m, l_m, acc_m, m_d, l_d, acc_d, q12_s, *, post_scale):
    tq = qm_ref.shape[2]
    tk = km_ref.shape[2]
    n_diag = tk // tq
    p = pl.program_id(1)
    kj = kj_ref[p]
    cls = cls_ref[p]

    @pl.when((pl.program_id(0) == 0) & (p == 0))
    def _build_bias():
        row = lax.broadcasted_iota(jnp.int32, (tq, tk), 0)
        col = lax.broadcasted_iota(jnp.int32, (tq, tk), 1)
        for near in range(n_diag + 1):
            qrow = row + (near * tq if near < n_diag else tk)
            bucket = _rel_bucket(col - qrow)
            masked = (col >> CHUNK_SHIFT) > (qrow >> CHUNK_SHIFT)
            if near < n_diag:
                mask_s[near] = jnp.where(masked, NEG, 0.0)

            def per_head(h, carry):
                far = tab_ref[REL_HALF - 1, h]
                b = (_bias_from_table(bucket, tab_ref, h) - far) * LOG2E
                bias_s[near, h] = jnp.where(masked, NEG, b) if near < n_diag else b
                return carry

            lax.fori_loop(0, HEADS, per_head, 0)

    @pl.when(kj == 0)
    def _init():
        m_m[...] = jnp.full(m_m.shape, NEG, F32)
        m_d[...] = jnp.full(m_d.shape, NEG, F32)
        l_m[...] = jnp.zeros(l_m.shape, F32)
        l_d[...] = jnp.zeros(l_d.shape, F32)
        acc_m[...] = jnp.zeros(acc_m.shape, F32)
        acc_d[...] = jnp.zeros(acc_d.shape, F32)
        for h in range(HEADS):
            q12_s[h] = _diff_queries(qd_ref[0, h])

    def step(near):
        keys = (near + 1) * tq if near is not None and near < n_diag else tk
        for h in range(HEADS):
            s = _dot_nt(qm_ref[0, h], km_ref[0, h, :keys, :])
            if near is not None and near < n_diag:
                s = s + mask_s[near, :, :keys]
            _online_softmax(s, vm_ref[0, h, :keys, :], m_m, l_m, acc_m, h)
            s12 = _dot_nt(q12_s[h], kd_ref[0, h, :keys, :])
            if near is not None:
                b = bias_s[near, h, :, :keys]
                s12 = s12 + jnp.concatenate([b, b], axis=0)
            _online_softmax(s12, vd_ref[0, h, :keys, :], m_d, l_d, acc_d, h)

    @pl.when(cls == n_diag + 1)
    def _far():
        step(None)

    @pl.when(cls == n_diag)
    def _before_diagonal():
        step(n_diag)

    for r in range(n_diag):
        @pl.when(cls == r)
        def _diagonal():
            step(r)
            lam = lam_ref[0:1, :]
            sg = sg_ref[...]

            def finish(h, carry):
                om_ref[0, h] = (acc_m[h] / l_m[h]).astype(om_ref.dtype)
                od_ref[0, h] = _diff_output(acc_d[h] / l_d[h], lam, sg, post_scale).astype(od_ref.dtype)
                return carry

            lax.fori_loop(0, HEADS, finish, 0)


def _attn_prompt(rel_table, qm, qd, km, vm, kd, vd, lamv, sg, tq, tk, post_scale):
    b, _, s, _ = qm.shape
    assert s % tk == 0 and tk % tq == 0 and tq % CHUNK == 0 and tq % LANES == 0 and tq + 1 >= REL_FAR
    n_diag = tk // tq
    qi, kj, cls = [], [], []
    for i in range(s // tq):
        jd, r = divmod(i, n_diag)
        for j in range(jd + 1):
            qi.append(i)
            kj.append(j)
            cls.append(r if j == jd else n_diag if (j == jd - 1 and r == 0) else n_diag + 1)
    tables = [jnp.asarray(np.array(a, np.int32)) for a in (qi, kj, cls)]
    qspec = pl.BlockSpec((1, HEADS, tq, LANES), lambda bi, p, qi_r, kj_r, c_r: (bi, 0, qi_r[p], 0))
    kspec = pl.BlockSpec((1, HEADS, tk, LANES), lambda bi, p, qi_r, kj_r, c_r: (bi, 0, kj_r[p], 0))
    stat = lambda rows: pltpu.VMEM((HEADS, rows, LANES), F32)
    out = jax.ShapeDtypeStruct((b, HEADS, s, LANES), BF16)
    return pl.pallas_call(
        functools.partial(_attn_kernel, post_scale=post_scale),
        grid_spec=pltpu.PrefetchScalarGridSpec(
            num_scalar_prefetch=3,
            grid=(b, len(qi)),
            in_specs=[pl.BlockSpec(memory_space=pltpu.SMEM),
                      qspec, qspec, kspec, kspec, kspec, kspec,
                      pl.BlockSpec((8, LANES), lambda bi, p, qi_r, kj_r, c_r: (0, 0)),
                      pl.BlockSpec((1, LANES), lambda bi, p, qi_r, kj_r, c_r: (0, 0))],
            out_specs=[qspec, qspec],
            scratch_shapes=[pltpu.VMEM((n_diag + 1, HEADS, tq, tk), F32), pltpu.VMEM((n_diag, tq, tk), F32),
                            stat(tq), stat(tq), stat(tq), stat(2 * tq), stat(2 * tq), stat(2 * tq),
                            pltpu.VMEM((HEADS, 2 * tq, LANES), BF16)],
        ),
        out_shape=[out, out],
        compiler_params=_params(("arbitrary", "arbitrary")),
        name="attn_prompt",
    )(*tables, rel_table, qm, qd, km, vm, kd, vd, lamv, sg)


def _attn_sample_kernel(tab_ref, qm_ref, qd_ref, kmn_ref, vmn_ref, kdn_ref, vdn_ref,
                        latc_ref, krc_ref, kdc_ref, vdc_ref, wk_ref, wv_ref, lam_ref, sg_ref,
                        om_ref, od_ref, *, post_scale):
    h = pl.program_id(1)
    nq = qm_ref.shape[2]
    past = latc_ref.shape[1]

    latb = latc_ref[0].astype(BF16)
    r = lax.broadcasted_iota(jnp.int32, (MLA_ROPE, LANES), 0)
    c = lax.broadcasted_iota(jnp.int32, (MLA_ROPE, LANES), 1)
    place = (c == r + ROPE_LO).astype(BF16)
    kc = (_dot(latb, wk_ref[...]) + _dot(krc_ref[0].astype(BF16), place)).astype(BF16)
    vc = _dot(latb, wv_ref[...]).astype(BF16)

    def geometry(reps, n_keys, key0):
        row = lax.broadcasted_iota(jnp.int32, (reps * nq, n_keys), 0)
        qpos = past + jnp.where(row >= nq, row - nq, row)
        kpos = key0 + lax.broadcasted_iota(jnp.int32, (reps * nq, n_keys), 1)
        visible = (kpos >> CHUNK_SHIFT) <= (qpos >> CHUNK_SHIFT)
        return visible, kpos - qpos

    def softmax_pv(s_c, s_n, v_c, v_n):
        m = jnp.maximum(jnp.max(s_c, axis=1, keepdims=True), jnp.max(s_n, axis=1, keepdims=True))
        p_c = jnp.exp2(s_c - m)
        p_n = jnp.exp2(s_n - m)
        l = jnp.sum(p_c, axis=1, keepdims=True) + jnp.sum(p_n, axis=1, keepdims=True)
        return (_dot(p_c.astype(BF16), v_c) + _dot(p_n.astype(BF16), v_n)) / l

    q = qm_ref[0, 0]
    vis_c, _ = geometry(1, past, 0)
    vis_n, _ = geometry(1, nq, past)
    s_c = jnp.where(vis_c, _dot_nt(q, kc), NEG)
    s_n = jnp.where(vis_n, _dot_nt(q, kmn_ref[0, 0]), NEG)
    om_ref[0, 0] = softmax_pv(s_c, s_n, vc, vmn_ref[0, 0]).astype(om_ref.dtype)

    q12 = _diff_queries(qd_ref[0, 0])
    vis_c, rel_c = geometry(2, past, 0)
    vis_n, rel_n = geometry(2, nq, past)
    b_c = _bias_from_table(_rel_bucket(rel_c), tab_ref, h) * LOG2E
    b_n = _bias_from_table(_rel_bucket(rel_n), tab_ref, h) * LOG2E
    s_c = jnp.where(vis_c, _dot_nt(q12, kdc_ref[0].astype(BF16)) + b_c, NEG)
    s_n = jnp.where(vis_n, _dot_nt(q12, kdn_ref[0, 0]) + b_n, NEG)
    o12 = softmax_pv(s_c, s_n, vdc_ref[0].astype(BF16), vdn_ref[0, 0])
    od_ref[0, 0] = _diff_output(o12, lam_ref[0:1, :], sg_ref[...], post_scale).astype(od_ref.dtype)


def _attn_sample(rel_table, qm, qd, kmn, vmn, kdn, vdn, latc, krc, kdc, vdc, wk, wv, lamv, sg, post_scale):
    b, _, nq, _ = qm.shape
    past = latc.shape[1]
    new = pl.BlockSpec((1, 1, nq, LANES), lambda bi, h: (bi, h, 0, 0))
    cache_head = pl.BlockSpec((1, past, LANES), lambda bi, h: (bi, 0, h))
    out = jax.ShapeDtypeStruct((b, HEADS, nq, LANES), BF16)
    return pl.pallas_call(
        functools.partial(_attn_sample_kernel, post_scale=post_scale),
        grid=(b, HEADS),
        in_specs=[pl.BlockSpec(memory_space=pltpu.SMEM), new, new, new, new, new, new,
                  pl.BlockSpec((1, past, MLA_KV_RANK), lambda bi, h: (bi, 0, 0)),
                  pl.BlockSpec((1, past, MLA_ROPE), lambda bi, h: (bi, 0, 0)),
                  cache_head, cache_head,
                  pl.BlockSpec((MLA_KV_RANK, LANES), lambda bi, h: (0, h)),
                  pl.BlockSpec((MLA_KV_RANK, LANES), lambda bi, h: (0, h)),
                  pl.BlockSpec((8, LANES), lambda bi, h: (0, 0)),
                  pl.BlockSpec((1, LANES), lambda bi, h: (0, 0))],
        out_specs=[new, new],
        out_shape=[out, out],
        compiler_params=_params(("parallel", "parallel")),
        name="attn_sample",
    )(rel_table, qm, qd, kmn, vmn, kdn, vdn, latc, krc, kdc, vdc, wk, wv, lamv, sg)


def _post_kernel(x_ref, om_ref, od_ref, g_ref, mod_ref, wom_ref, wod_ref, wout_ref, n2_ref,
                 wrh_ref, wrl_ref, br_ref, x1_ref, h2_ref, gate_ref):
    d = x_ref.shape[2]
    om = jnp.concatenate([om_ref[0, hd] for hd in range(HEADS)], axis=1)
    od = jnp.concatenate([od_ref[0, hd] for hd in range(HEADS)], axis=1)
    g = g_ref[0].astype(F32)
    merged = g[:, :d] * _dot(om, wom_ref[...]) + g[:, d:] * _dot(od, wod_ref[...])
    m = mod_ref[0]
    x1 = x_ref[0] + m[2:3] * _dot(merged.astype(BF16), wout_ref[...])
    x1_ref[0] = x1
    h2 = _rms(x1, n2_ref[...]) * (1.0 + m[4:5]) + m[3:4]
    h2_ref[0] = h2.astype(BF16)

    hi, lo = _split_bf16(h2)
    wrh = wrh_ref[...]
    logits = _dot(hi, wrh) + (_dot(hi, wrl_ref[...]) + _dot(lo, wrh)) + br_ref[...]
    lane = lax.broadcasted_iota(jnp.int32, logits.shape, 1).astype(F32)
    work = logits
    tops, picks = [], []
    for _ in range(TOP_K):
        top = jnp.max(work, axis=1, keepdims=True)
        first = jnp.min(jnp.where(work == top, lane, float(N_EXPERTS)), axis=1, keepdims=True)
        pick = lane == first
        tops.append(top)
        picks.append(pick)
        work = jnp.where(pick, -jnp.inf, work)
    es = [jnp.exp(top - tops[0]) for top in tops]
    den = es[0]
    for e in es[1:]:
        den = den + e
    gate = jnp.zeros(logits.shape, F32)
    for e, pick in zip(es, picks):
        gate = jnp.where(pick, e / den, gate)
    gate_ref[0] = gate


def _post(x, om, od, g, mod, tt, w):
    b, s, d = x.shape
    tok = lambda width: pl.BlockSpec((1, tt, width), lambda bi, i: (bi, i, 0))
    head = pl.BlockSpec((1, HEADS, tt, LANES), lambda bi, i: (bi, 0, i, 0))
    weights = [w["wom"], w["wod"], w["wout"], w["n2"], w["wrh"], w["wrl"], w["br"]]
    return pl.pallas_call(
        _post_kernel,
        grid=(b, s // tt),
        in_specs=[tok(d), head, head, tok(2 * d), pl.BlockSpec((1, 6, d), lambda bi, i: (bi, 0, 0))]
                 + [_const_spec(a.shape) for a in weights],
        out_specs=[tok(d), tok(d), tok(N_EXPERTS)],
        out_shape=[jax.ShapeDtypeStruct((b, s, d), F32),
                   jax.ShapeDtypeStruct((b, s, d), BF16),
                   jax.ShapeDtypeStruct((b, s, N_EXPERTS), F32)],
        compiler_params=_params(("parallel", "parallel")),
        name="post",
    )(x, om, od, g, mod, *weights)


def _moe_kernel(h_ref, gate_ref, win_ref, bin_ref, wdn_ref, bdn_ref, f_ref, acc_ref, *, f_chunk):
    e = pl.program_id(1)
    gate = gate_ref[...]
    d_exp = wdn_ref.shape[1]

    @pl.when(e == 0)
    def _():
        hi, lo = _split_bf16(gate)
        bdn = bdn_ref[...]
        acc_ref[...] = _dot(hi, bdn) + _dot(lo, bdn)

    lane = lax.broadcasted_iota(jnp.int32, gate.shape, 1)
    ge = jnp.sum(jnp.where(lane == e, gate, 0.0), axis=1, keepdims=True)
    h = h_ref[...]
    for c0 in range(0, d_exp, f_chunk):
        c1 = c0 + f_chunk
        gl = _dot(h, win_ref[0, :, c0:c1]) + bin_ref[0, :, c0:c1]
        up = _dot(h, win_ref[0, :, d_exp + c0:d_exp + c1]) + bin_ref[0, :, d_exp + c0:d_exp + c1]
        gl = jnp.minimum(gl, SWIGLU_LIMIT)
        up = jnp.clip(up, -SWIGLU_LIMIT, SWIGLU_LIMIT)
        act = gl * jax.nn.sigmoid(SWIGLU_ALPHA * gl) * (up + 1.0) * ge
        acc_ref[...] += _dot(act.astype(BF16), wdn_ref[0, c0:c1, :])

    @pl.when(e == pl.num_programs(1) - 1)
    def _():
        f_ref[...] = acc_ref[...]


def _moe(h2, gate, w, tm):
    n, d = h2.shape
    n_exp, _, two_f = w["win"].shape
    d_exp = two_f // 2
    assert n % tm == 0
    return pl.pallas_call(
        functools.partial(_moe_kernel, f_chunk=min(512, d_exp)),
        grid=(n // tm, n_exp),
        in_specs=[pl.BlockSpec((tm, d), lambda i, e: (i, 0)),
                  pl.BlockSpec((tm, n_exp), lambda i, e: (i, 0)),
                  pl.BlockSpec((1, d, two_f), lambda i, e: (e, 0, 0)),
                  pl.BlockSpec((1, 1, two_f), lambda i, e: (e, 0, 0)),
                  pl.BlockSpec((1, d_exp, d), lambda i, e: (e, 0, 0)),
                  pl.BlockSpec((n_exp, d), lambda i, e: (0, 0))],
        out_specs=pl.BlockSpec((tm, d), lambda i, e: (i, 0)),
        out_shape=jax.ShapeDtypeStruct((n, d), F32),
        scratch_shapes=[pltpu.VMEM((tm, d), F32)],
        compiler_params=_params(("parallel", "arbitrary")),
        name="moe",
    )(h2, gate, w["win"], w["bin"], w["wdn"], w["bdn"])


def _final_kernel(x1_ref, f_ref, mod_ref, g_ref, y_ref, *, normalize):
    y = x1_ref[0] + mod_ref[0][5:6] * f_ref[0]
    y_ref[0] = _rms(y, g_ref[...]) if normalize else y


def _final(x1, f, mod, g, tt, normalize):
    b, s, d = x1.shape
    tok = pl.BlockSpec((1, tt, d), lambda bi, i: (bi, i, 0))
    return pl.pallas_call(
        functools.partial(_final_kernel, normalize=normalize),
        grid=(b, s // tt),
        in_specs=[tok, tok, pl.BlockSpec((1, 6, d), lambda bi, i: (bi, 0, 0)),
                  pl.BlockSpec((1, d), lambda bi, i: (0, 0))],
        out_specs=tok,
        out_shape=jax.ShapeDtypeStruct((b, s, d), F32),
        compiler_params=_params(("parallel", "parallel")),
        name="final",
    )(x1, f, mod, g)


def _layer_weights(l, w_in, q_norm_g, w_uq, kv_norm_g, w_ukv, norm1_g, subln_g, w_o_mla, w_o_diff, w_out,
                   norm2_g, w_router, b_router, w_e_in, b_e_in, w_e_down, b_e_down):
    d = w_in.shape[1]
    hw = HEADS * LANES
    sizes = (MLA_Q_RANK, MLA_KV_RANK, MLA_ROPE, hw, hw, hw, 2 * d)
    offs = np.concatenate([[0], np.cumsum(sizes)])
    pieces = [w_in[l][:, offs[i]:offs[i + 1]] for i in range(len(sizes))]
    w_cq, w_ckv, w_kr, w_dq, w_dk, w_dv, w_g = pieces
    half = MLA_ROPE // 2
    pad_lo = jnp.zeros((d, ROPE_LO), F32)
    pad_hi = jnp.zeros((d, LANES - ROPE_HI), F32)
    kr_lanes = jnp.concatenate([pad_lo, w_kr, pad_hi], axis=1)
    kr_swap = jnp.concatenate([pad_lo, w_kr[:, half:], w_kr[:, :half], pad_hi], axis=1)
    w1 = jnp.concatenate([w_cq, w_ckv, kr_lanes, kr_swap], axis=1)

    uq = w_uq[l].reshape(MLA_Q_RANK, HEADS, MLA_NOPE + MLA_ROPE)
    nope, rope = uq[..., :MLA_NOPE], uq[..., MLA_NOPE:]
    zq = lambda n: jnp.zeros((MLA_Q_RANK, HEADS, n), F32)
    wq = jnp.concatenate([nope, rope, zq(LANES - ROPE_HI)], axis=-1).reshape(MLA_Q_RANK, hw)
    wqs = jnp.concatenate([zq(ROPE_LO), rope[..., half:], rope[..., :half], zq(LANES - ROPE_HI)],
                          axis=-1).reshape(MLA_Q_RANK, hw)
    ukv = w_ukv[l].reshape(MLA_KV_RANK, HEADS, MLA_NOPE + MLA_V)
    zk = lambda n: jnp.zeros((MLA_KV_RANK, HEADS, n), F32)
    wk = jnp.concatenate([ukv[..., :MLA_NOPE], zk(LANES - MLA_NOPE)], axis=-1).reshape(MLA_KV_RANK, hw)
    wv = jnp.concatenate([ukv[..., MLA_NOPE:], zk(LANES - MLA_V)], axis=-1).reshape(MLA_KV_RANK, hw)
    wom = jnp.concatenate([w_o_mla[l].reshape(HEADS, MLA_V, d), jnp.zeros((HEADS, LANES - MLA_V, d), F32)],
                          axis=1).reshape(hw, d)
    wrh, wrl = _split_bf16(w_router[l])
    bf = lambda a: a.astype(BF16)
    row = lambda a: a.reshape(1, -1)
    return dict(
        n1=row(norm1_g[l]), qg=row(q_norm_g[l]), kvg=row(kv_norm_g[l]),
        w1=bf(w1), wq=bf(wq), wqs=bf(wqs), wk=bf(wk), wv=bf(wv),
        wdq=bf(w_dq), wdk=bf(w_dk), wdv=bf(w_dv), wg=bf(w_g),
        sg=row(subln_g[l]), wom=bf(wom), wod=bf(w_o_diff[l]), wout=bf(w_out[l]),
        n2=row(norm2_g[l]), wrh=wrh, wrl=wrl, br=row(b_router[l]),
        win=bf(w_e_in[l]), bin=b_e_in[l][:, None, :], wdn=bf(w_e_down[l]), bdn=bf(b_e_down[l]),
    )


def _tile(n, target):
    t = min(n, target)
    while n % t:
        t //= 2
    return t


def kernel(x_prompt, x_sample, cache_mla_latent, cache_mla_krope, cache_diff_k, cache_diff_v,
           c_prompt, c_sample, rel_table, final_norm_g, w_ada, b_ada, norm1_g, w_in, q_norm_g, w_uq,
           kv_norm_g, w_ukv, lam_q1, lam_k1, lam_q2, lam_k2, subln_g, w_o_mla, w_o_diff, w_out,
           norm2_g, w_router, b_router, w_e_in, b_e_in, w_e_down, b_e_down):
    depth = w_ada.shape[0]
    bp, sp, d = x_prompt.shape
    bs, ss, _ = x_sample.shape
    past = cache_mla_latent.shape[2]
    assert HEADS * DIFF_VD == HEADS * LANES and past % ss == 0 and ss % SUBLANES_BF16 == 0

    n_pos = max(sp, past + ss)
    rope_rows = _tile(n_pos, 2048)
    cos_tab, sin_tab = _rope_tables(-(-n_pos // rope_rows) * rope_rows, rope_rows)

    pad_rows = -(bp + bs) % SUBLANES_BF16
    c_all = jnp.concatenate([c_prompt, c_sample, jnp.zeros((pad_rows, d), F32)], axis=0)

    g_fin = final_norm_g.reshape(1, d)
    hp, hs = x_prompt, x_sample
    rows_p, rows_s = [], []
    for l in range(depth):
        lam_init = 0.8 - 0.6 * math.exp(-0.3 * l)
        w = _layer_weights(l, w_in, q_norm_g, w_uq, kv_norm_g, w_ukv, norm1_g, subln_g, w_o_mla, w_o_diff,
                           w_out, norm2_g, w_router, b_router, w_e_in, b_e_in, w_e_down, b_e_down)
        lam_vecs = jnp.stack([lam_q1[l], lam_k1[l], lam_q2[l], lam_k2[l]])
        mod, lamv = _ada(c_all, w_ada[l].astype(BF16), b_ada[l].reshape(1, -1), lam_vecs, lam_init)
        mod_p = mod[:bp].reshape(bp, 6, d)
        mod_s = mod[bp:bp + bs].reshape(bs, 6, d)
        post_scale = 1.0 - lam_init

        tt = _tile(sp, 256)
        lat, kr, dkf, dvf, qm, km, vm, qd, kd, vd, g = _proj(hp, mod_p, cos_tab, sin_tab, 0, tt, w)
        om, od = _attn_prompt(rel_table, qm, qd, km, vm, kd, vd, lamv, w["sg"], _tile(sp, 256), _tile(sp, 256), post_scale)
        x1, h2, gate = _post(hp, om, od, g, mod_p, tt, w)
        f = _moe(h2.reshape(bp * sp, d), gate.reshape(bp * sp, N_EXPERTS), w, _tile(bp * sp, 1024))
        hp = (x1, f.reshape(bp, sp, d), mod_p)
        rows_p.append((lat, kr, dkf.reshape(bp, sp, HEADS, 2 * DIFF_HD), dvf.reshape(bp, sp, HEADS, DIFF_VD)))

        lat, kr, dkf, dvf, qm, km, vm, qd, kd, vd, g = _proj(hs, mod_s, cos_tab, sin_tab, past, ss, w)
        om, od = _attn_sample(rel_table, qm, qd, km, vm, kd, vd,
                              cache_mla_latent[l], cache_mla_krope[l],
                              cache_diff_k[l].reshape(bs, past, HEADS * 2 * DIFF_HD),
                              cache_diff_v[l].reshape(bs, past, HEADS * DIFF_VD),
                              w["wk"], w["wv"], lamv, w["sg"], post_scale)
        x1, h2, gate = _post(hs, om, od, g, mod_s, ss, w)
        f = _moe(h2.reshape(bs * ss, d), gate.reshape(bs * ss, N_EXPERTS), w, bs * ss)
        hs = (x1, f.reshape(bs, ss, d), mod_s)
        rows_s.append((lat, kr, dkf.reshape(bs, ss, HEADS, 2 * DIFF_HD), dvf.reshape(bs, ss, HEADS, DIFF_VD)))

        if l + 1 < depth:
            hp = _final(*hp, g_fin, _tile(sp, 1024), normalize=False)
            hs = _final(*hs, g_fin, ss, normalize=False)

    y_prompt = _final(*hp, g_fin, _tile(sp, 1024), normalize=True)
    y_sample = _final(*hs, g_fin, ss, normalize=True)
    stack = lambda rows, i: jnp.stack([r[i] for r in rows])
    return (y_prompt, y_sample,
            stack(rows_p, 0), stack(rows_p, 1), stack(rows_p, 2), stack(rows_p, 3),
            stack(rows_s, 0), stack(rows_s, 1), stack(rows_s, 2), stack(rows_s, 3))
```

```python
import functools
import math

import jax
import jax.numpy as jnp
import numpy as np
from jax import lax
from jax.experimental import pallas as pl
from jax.experimental.pallas import tpu as pltpu

F32 = jnp.float32
BF16 = jnp.bfloat16

CHUNK = 64
EPS = 1e-6
HEADS = 8
MLA_Q_RANK = 384
MLA_KV_RANK = 256
MLA_NOPE = 64
MLA_ROPE = 32
MLA_V = 64
MLA_SCALE = (MLA_NOPE + MLA_ROPE) ** -0.5
ROPE_THETA = 10000.0
DIFF_HD = 64
DIFF_VD = 2 * DIFF_HD
DIFF_SCALE = DIFF_HD ** -0.5
REL_BUCKETS = 32
REL_MAX_DIST = 128
N_EXPERTS = 32
TOP_K = 4
SWIGLU_LIMIT = 7.0
SWIGLU_ALPHA = 1.702
NEG = -1e30
LOG2E = math.log2(math.e)

LANES = 128
SUBLANES_BF16 = 16
VMEM_LIMIT = 56 * 1024 * 1024

ROPE_LO = MLA_NOPE
ROPE_HI = MLA_NOPE + MLA_ROPE
CHUNK_SHIFT = CHUNK.bit_length() - 1
assert 1 << CHUNK_SHIFT == CHUNK

REL_HALF = REL_BUCKETS // 2
REL_EXACT = REL_HALF // 2
assert (REL_MAX_DIST // REL_EXACT) == 1 << (REL_HALF - REL_EXACT) // 2
REL_FAR = math.isqrt(REL_EXACT * REL_EXACT * 2 ** (REL_HALF - REL_EXACT - 1) - 1) + 1


def _params(sem, vmem=VMEM_LIMIT):
    return pltpu.CompilerParams(dimension_semantics=sem, vmem_limit_bytes=vmem)


def _const_spec(shape):
    zeros = (0,) * len(shape)
    return pl.BlockSpec(shape, lambda *_: zeros, pipeline_mode=pl.Buffered(1))


def _dot(a, b):
    return jnp.dot(a, b, preferred_element_type=F32)


def _dot_nt(a, b):
    return lax.dot_general(a, b, (((1,), (1,)), ((), ())), preferred_element_type=F32)


def _rms(x, g):
    return x * lax.rsqrt(jnp.mean(x * x, axis=-1, keepdims=True) + EPS) * g


def _split_bf16(x):
    hi = x.astype(BF16)
    lo = (x - hi.astype(F32)).astype(BF16)
    return hi, lo


def _rel_bucket(rel):
    n = jnp.abs(rel)
    n2 = n * n
    cnt = jnp.zeros_like(n)
    for j in range(1, REL_HALF - REL_EXACT):
        cnt = cnt + (n2 >= REL_EXACT * REL_EXACT * 2 ** j).astype(jnp.int32)
    large = REL_EXACT + cnt
    return jnp.where(rel > 0, REL_HALF, 0) + jnp.where(n < REL_EXACT, n, large)


def _bias_from_table(bucket, tab_ref, h):
    out = jnp.zeros(bucket.shape, F32)
    for b in range(REL_BUCKETS):
        out = jnp.where(bucket == b, tab_ref[b, h], out)
    return out


def _ada_kernel(c_ref, w_ref, b_ref, lamv_ref, mod_ref, lam_ref, *, lam_init):
    c = c_ref[...]
    s = (c * jax.nn.sigmoid(c)).astype(BF16)
    mod_ref[...] = _dot(s, w_ref[...]) + b_ref[...]

    @pl.when(pl.program_id(0) == 0)
    def _():
        v = lamv_ref[...]
        a = jnp.sum(v[0:1] * v[1:2], axis=1, keepdims=True)
        b = jnp.sum(v[2:3] * v[3:4], axis=1, keepdims=True)
        lam_ref[...] = jnp.broadcast_to(jnp.exp(a) - jnp.exp(b) + lam_init, lam_ref.shape)


def _ada(c_all, w_ada, b_ada, lam_vecs, lam_init):
    rows, d = c_all.shape
    n = w_ada.shape[1]
    tn = d
    return pl.pallas_call(
        functools.partial(_ada_kernel, lam_init=lam_init),
        grid=(n // tn,),
        in_specs=[
            pl.BlockSpec((rows, d), lambda j: (0, 0)),
            pl.BlockSpec((d, tn), lambda j: (0, j)),
            pl.BlockSpec((1, tn), lambda j: (0, j)),
            pl.BlockSpec(lam_vecs.shape, lambda j: (0, 0)),
        ],
        out_specs=[
            pl.BlockSpec((rows, tn), lambda j: (0, j)),
            pl.BlockSpec((8, LANES), lambda j: (0, 0)),
        ],
        out_shape=[
            jax.ShapeDtypeStruct((rows, n), F32),
            jax.ShapeDtypeStruct((8, LANES), F32),
        ],
        compiler_params=_params(("arbitrary",)),
        name="ada",
    )(c_all, w_ada, b_ada, lam_vecs)


def _rope_kernel(inv_ref, cos_ref, sin_ref):
    rows = cos_ref.shape[0]
    pos = pl.program_id(0) * rows + lax.broadcasted_iota(jnp.int32, (rows, LANES), 0)
    lane = lax.broadcasted_iota(jnp.int32, (rows, LANES), 1)
    ang = pos.astype(F32) * inv_ref[...]
    in_rope = (lane >= ROPE_LO) & (lane < ROPE_HI)
    first_half = lane < ROPE_LO + MLA_ROPE // 2
    cos_ref[...] = jnp.where(lane < ROPE_LO, 1.0, jnp.where(in_rope, jnp.cos(ang), 0.0))
    sn = jnp.sin(ang)
    sin_ref[...] = jnp.where(in_rope, jnp.where(first_half, -sn, sn), 0.0)


def _rope_tables(n_pos, rows):
    half = MLA_ROPE // 2
    inv = ROPE_THETA ** (-jnp.arange(half, dtype=F32) / half)
    inv_lanes = jnp.zeros((1, LANES), F32).at[0, ROPE_LO:ROPE_HI].set(jnp.concatenate([inv, inv]))
    return pl.pallas_call(
        _rope_kernel,
        grid=(n_pos // rows,),
        in_specs=[pl.BlockSpec((1, LANES), lambda i: (0, 0))],
        out_specs=[pl.BlockSpec((rows, LANES), lambda i: (i, 0))] * 2,
        out_shape=[jax.ShapeDtypeStruct((n_pos, LANES), F32)] * 2,
        compiler_params=_params(("parallel",)),
        name="rope_tables",
    )(inv_lanes)


def _proj_kernel(x_ref, mod_ref, cos_ref, sin_ref, n1_ref, qg_ref, kvg_ref,
                 w1_ref, wq_ref, wqs_ref, wk_ref, wv_ref, wdq_ref, wdk_ref, wdv_ref, wg_ref,
                 lat_ref, kr_ref, dkf_ref, dvf_ref,
                 qm_ref, km_ref, vm_ref, qd_ref, kd_ref, vd_ref, g_ref):
    m = mod_ref[0]
    h = (_rms(x_ref[0], n1_ref[...]) * (1.0 + m[1:2]) + m[0:1]).astype(BF16)
    cos = cos_ref[...]
    sin = sin_ref[...]

    z1 = _dot(h, w1_ref[...])
    r0 = MLA_Q_RANK
    r1 = r0 + MLA_KV_RANK
    cq, ckv = z1[:, :r0], z1[:, r0:r1]
    kr, krs = z1[:, r1:r1 + LANES], z1[:, r1 + LANES:]

    qc = _rms(cq, qg_ref[...]).astype(BF16)
    q = _dot(qc, wq_ref[...])
    qs = _dot(qc, wqs_ref[...])
    lat = _rms(ckv, kvg_ref[...])
    lat_ref[0] = lat
    latb = lat.astype(BF16)
    krope = kr * cos + krs * sin
    kr_ref[0] = krope[:, ROPE_LO:ROPE_HI]
    kn = _dot(latb, wk_ref[...])
    v = _dot(latb, wv_ref[...])
    dq = _dot(h, wdq_ref[...]) * (DIFF_SCALE * LOG2E)
    dk = _dot(h, wdk_ref[...])
    dv = _dot(h, wdv_ref[...])
    dkf_ref[0] = dk
    dvf_ref[0] = dv
    sum_lane = (lax.broadcasted_iota(jnp.int32, (1, LANES), 1) == MLA_V).astype(F32)
    for hd in range(HEADS):
        sl = slice(hd * LANES, (hd + 1) * LANES)
        qm_ref[0, hd] = ((q[:, sl] * cos + qs[:, sl] * sin) * (MLA_SCALE * LOG2E)).astype(BF16)
        km_ref[0, hd] = (kn[:, sl] + krope).astype(BF16)
        vm_ref[0, hd] = (v[:, sl] + sum_lane).astype(BF16)
        qd_ref[0, hd] = dq[:, sl].astype(BF16)
        kd_ref[0, hd] = dk[:, sl].astype(BF16)
        vd_ref[0, hd] = dv[:, sl].astype(BF16)
    g_ref[0] = jax.nn.sigmoid(_dot(h, wg_ref[...])).astype(BF16)


def _proj(x, mod, cos_tab, sin_tab, pos0, tt, w):
    b, s, d = x.shape
    assert s % tt == 0 and pos0 % tt == 0
    off = pos0 // tt
    tok = lambda width: pl.BlockSpec((1, tt, width), lambda bi, i: (bi, i, 0))
    head = pl.BlockSpec((1, HEADS, tt, LANES), lambda bi, i: (bi, 0, i, 0))
    tab = pl.BlockSpec((tt, LANES), lambda bi, i: (off + i, 0))
    hm = jax.ShapeDtypeStruct((b, HEADS, s, LANES), BF16)
    weights = [w["n1"], w["qg"], w["kvg"], w["w1"], w["wq"], w["wqs"], w["wk"], w["wv"],
               w["wdq"], w["wdk"], w["wdv"], w["wg"]]
    return pl.pallas_call(
        _proj_kernel,
        grid=(b, s // tt),
        in_specs=[tok(d), pl.BlockSpec((1, 6, d), lambda bi, i: (bi, 0, 0)), tab, tab]
                 + [_const_spec(a.shape) for a in weights],
        out_specs=[tok(MLA_KV_RANK), tok(MLA_ROPE), tok(HEADS * LANES), tok(HEADS * LANES)]
                  + [head] * 6 + [tok(2 * d)],
        out_shape=[jax.ShapeDtypeStruct((b, s, MLA_KV_RANK), F32),
                   jax.ShapeDtypeStruct((b, s, MLA_ROPE), F32),
                   jax.ShapeDtypeStruct((b, s, HEADS * LANES), F32),
                   jax.ShapeDtypeStruct((b, s, HEADS * LANES), F32)]
                  + [hm] * 6 + [jax.ShapeDtypeStruct((b, s, 2 * d), BF16)],
        compiler_params=_params(("parallel", "parallel")),
        name="proj",
    )(x, mod, cos_tab, sin_tab, *weights)


def _online_softmax(s, v, m_ref, l_ref, acc_ref, h):
    m_prev = m_ref[h]
    m_new = jnp.maximum(m_prev, jnp.max(s, axis=1, keepdims=True))
    alpha = jnp.exp2(m_prev - m_new)
    p = jnp.exp2(s - jnp.tile(m_new, (1, s.shape[1] // LANES)))
    if l_ref is not None:
        l_ref[h] = alpha * l_ref[h] + jnp.sum(p, axis=1, keepdims=True)
    acc_ref[h] = alpha * acc_ref[h] + _dot(p.astype(BF16), v)
    m_ref[h] = m_new


def _diff_queries(qd):
    lane = lax.broadcasted_iota(jnp.int32, (1, LANES), 1)
    first = (lane < DIFF_HD).astype(qd.dtype)
    return jnp.concatenate([qd * first, qd * (1 - first)], axis=0)


def _diff_output(o12, lam, sg, post_scale):
    rows = o12.shape[0] // 2
    o = o12[:rows] - lam * o12[rows:]
    return _rms(o, sg) * post_scale


def _attn_kernel(qi_ref, kj_ref, cls_ref, tab_ref, qm_ref, qd_ref, km_ref, vm_ref, kd_ref, vd_ref, lam_ref, sg_ref,
                 om_ref, od_ref,
                 bias_s, mask_s, m_m, acc_m, m_d, l_d, acc_d, q12_s, *, post_scale):
    tq = qm_ref.shape[2]
    tk = km_ref.shape[2]
    n_diag = tk // tq
    p = pl.program_id(1)
    kj = kj_ref[p]
    cls = cls_ref[p]

    @pl.when((pl.program_id(0) == 0) & (p == 0))
    def _build_bias():
        row = lax.broadcasted_iota(jnp.int32, (tq, tk), 0)
        col = lax.broadcasted_iota(jnp.int32, (tq, tk), 1)
        for near in range(n_diag + 1):
            qrow = row + (near * tq if near < n_diag else tk)
            bucket = _rel_bucket(col - qrow)
            masked = (col >> CHUNK_SHIFT) > (qrow >> CHUNK_SHIFT)
            if near < n_diag:
                mask_s[near] = jnp.where(masked, NEG, 0.0)

            def per_head(h, carry):
                far = tab_ref[REL_HALF - 1, h]
                b = (_bias_from_table(bucket, tab_ref, h) - far) * LOG2E
                bias_s[near, h] = jnp.where(masked, NEG, b) if near < n_diag else b
                return carry

            lax.fori_loop(0, HEADS, per_head, 0)

    @pl.when(kj == 0)
    def _init():
        m_m[...] = jnp.full(m_m.shape, NEG, F32)
        m_d[...] = jnp.full(m_d.shape, NEG, F32)
        l_d[...] = jnp.zeros(l_d.shape, F32)
        acc_m[...] = jnp.zeros(acc_m.shape, F32)
        acc_d[...] = jnp.zeros(acc_d.shape, F32)
        for h in range(HEADS):
            q12_s[h] = _diff_queries(qd_ref[0, h])

    def step(near):
        keys = (near + 1) * tq if near is not None and near < n_diag else tk
        for h in range(HEADS):
            s = _dot_nt(qm_ref[0, h], km_ref[0, h, :keys, :])
            if near is not None and near < n_diag:
                s = s + mask_s[near, :, :keys]
            _online_softmax(s, vm_ref[0, h, :keys, :], m_m, None, acc_m, h)
            s12 = _dot_nt(q12_s[h], kd_ref[0, h, :keys, :])
            if near is not None:
                b = bias_s[near, h, :, :keys]
                s12 = s12 + jnp.concatenate([b, b], axis=0)
            _online_softmax(s12, vd_ref[0, h, :keys, :], m_d, l_d, acc_d, h)

    @pl.when(cls == n_diag + 1)
    def _far():
        step(None)

    @pl.when(cls == n_diag)
    def _before_diagonal():
        step(n_diag)

    for r in range(n_diag):
        @pl.when(cls == r)
        def _diagonal():
            step(r)
            lam = lam_ref[0:1, :]
            sg = sg_ref[...]

            def finish(h, carry):
                a = acc_m[h]
                om_ref[0, h] = (a / a[:, MLA_V:MLA_V + 1]).astype(om_ref.dtype)
                od_ref[0, h] = _diff_output(acc_d[h] / l_d[h], lam, sg, post_scale).astype(od_ref.dtype)
                return carry

            lax.fori_loop(0, HEADS, finish, 0)


def _attn_prompt(rel_table, qm, qd, km, vm, kd, vd, lamv, sg, tq, tk, post_scale):
    b, _, s, _ = qm.shape
    assert s % tk == 0 and tk % tq == 0 and tq % CHUNK == 0 and tq % LANES == 0 and tq + 1 >= REL_FAR
    n_diag = tk // tq
    qi, kj, cls = [], [], []
    for i in range(s // tq):
        jd, r = divmod(i, n_diag)
        for j in range(jd + 1):
            qi.append(i)
            kj.append(j)
            cls.append(r if j == jd else n_diag if (j == jd - 1 and r == 0) else n_diag + 1)
    tables = [jnp.asarray(np.array(a, np.int32)) for a in (qi, kj, cls)]
    qspec = pl.BlockSpec((1, HEADS, tq, LANES), lambda bi, p, qi_r, kj_r, c_r: (bi, 0, qi_r[p], 0))
    kspec = pl.BlockSpec((1, HEADS, tk, LANES), lambda bi, p, qi_r, kj_r, c_r: (bi, 0, kj_r[p], 0))
    stat = lambda rows: pltpu.VMEM((HEADS, rows, LANES), F32)
    out = jax.ShapeDtypeStruct((b, HEADS, s, LANES), BF16)
    return pl.pallas_call(
        functools.partial(_attn_kernel, post_scale=post_scale),
        grid_spec=pltpu.PrefetchScalarGridSpec(
            num_scalar_prefetch=3,
            grid=(b, len(qi)),
            in_specs=[pl.BlockSpec(memory_space=pltpu.SMEM),
                      qspec, qspec, kspec, kspec, kspec, kspec,
                      pl.BlockSpec((8, LANES), lambda bi, p, qi_r, kj_r, c_r: (0, 0)),
                      pl.BlockSpec((1, LANES), lambda bi, p, qi_r, kj_r, c_r: (0, 0))],
            out_specs=[qspec, qspec],
            scratch_shapes=[pltpu.VMEM((n_diag + 1, HEADS, tq, tk), F32), pltpu.VMEM((n_diag, tq, tk), F32),
                            stat(tq), stat(tq), stat(2 * tq), stat(2 * tq), stat(2 * tq),
                            pltpu.VMEM((HEADS, 2 * tq, LANES), BF16)],
        ),
        out_shape=[out, out],
        compiler_params=_params(("arbitrary", "arbitrary")),
        name="attn_prompt",
    )(*tables, rel_table, qm, qd, km, vm, kd, vd, lamv, sg)


def _attn_sample_kernel(tab_ref, qm_ref, qd_ref, kmn_ref, vmn_ref, kdn_ref, vdn_ref,
                        latc_ref, krc_ref, kdc_ref, vdc_ref, wk_ref, wv_ref, lam_ref, sg_ref,
                        om_ref, od_ref, *, post_scale):
    h = pl.program_id(1)
    nq = qm_ref.shape[2]
    past = latc_ref.shape[1]

    latb = latc_ref[0].astype(BF16)
    r = lax.broadcasted_iota(jnp.int32, (MLA_ROPE, LANES), 0)
    c = lax.broadcasted_iota(jnp.int32, (MLA_ROPE, LANES), 1)
    place = (c == r + ROPE_LO).astype(BF16)
    kc = (_dot(latb, wk_ref[...]) + _dot(krc_ref[0].astype(BF16), place)).astype(BF16)
    vc = _dot(latb, wv_ref[...]).astype(BF16)

    def geometry(reps, n_keys, key0):
        row = lax.broadcasted_iota(jnp.int32, (reps * nq, n_keys), 0)
        qpos = past + jnp.where(row >= nq, row - nq, row)
        kpos = key0 + lax.broadcasted_iota(jnp.int32, (reps * nq, n_keys), 1)
        visible = (kpos >> CHUNK_SHIFT) <= (qpos >> CHUNK_SHIFT)
        return visible, kpos - qpos

    def softmax_pv(s_c, s_n, v_c, v_n):
        m = jnp.maximum(jnp.max(s_c, axis=1, keepdims=True), jnp.max(s_n, axis=1, keepdims=True))
        p_c = jnp.exp2(s_c - m)
        p_n = jnp.exp2(s_n - m)
        l = jnp.sum(p_c, axis=1, keepdims=True) + jnp.sum(p_n, axis=1, keepdims=True)
        return (_dot(p_c.astype(BF16), v_c) + _dot(p_n.astype(BF16), v_n)) / l

    q = qm_ref[0, 0]
    vis_c, _ = geometry(1, past, 0)
    vis_n, _ = geometry(1, nq, past)
    s_c = jnp.where(vis_c, _dot_nt(q, kc), NEG)
    s_n = jnp.where(vis_n, _dot_nt(q, kmn_ref[0, 0]), NEG)
    om_ref[0, 0] = softmax_pv(s_c, s_n, vc, vmn_ref[0, 0]).astype(om_ref.dtype)

    q12 = _diff_queries(qd_ref[0, 0])
    vis_c, rel_c = geometry(2, past, 0)
    vis_n, rel_n = geometry(2, nq, past)
    b_c = _bias_from_table(_rel_bucket(rel_c), tab_ref, h) * LOG2E
    b_n = _bias_from_table(_rel_bucket(rel_n), tab_ref, h) * LOG2E
    s_c = jnp.where(vis_c, _dot_nt(q12, kdc_ref[0].astype(BF16)) + b_c, NEG)
    s_n = jnp.where(vis_n, _dot_nt(q12, kdn_ref[0, 0]) + b_n, NEG)
    o12 = softmax_pv(s_c, s_n, vdc_ref[0].astype(BF16), vdn_ref[0, 0])
    od_ref[0, 0] = _diff_output(o12, lam_ref[0:1, :], sg_ref[...], post_scale).astype(od_ref.dtype)


def _attn_sample(rel_table, qm, qd, kmn, vmn, kdn, vdn, latc, krc, kdc, vdc, wk, wv, lamv, sg, post_scale):
    b, _, nq, _ = qm.shape
    past = latc.shape[1]
    new = pl.BlockSpec((1, 1, nq, LANES), lambda bi, h: (bi, h, 0, 0))
    cache_head = pl.BlockSpec((1, past, LANES), lambda bi, h: (bi, 0, h))
    out = jax.ShapeDtypeStruct((b, HEADS, nq, LANES), BF16)
    return pl.pallas_call(
        functools.partial(_attn_sample_kernel, post_scale=post_scale),
        grid=(b, HEADS),
        in_specs=[pl.BlockSpec(memory_space=pltpu.SMEM), new, new, new, new, new, new,
                  pl.BlockSpec((1, past, MLA_KV_RANK), lambda bi, h: (bi, 0, 0)),
                  pl.BlockSpec((1, past, MLA_ROPE), lambda bi, h: (bi, 0, 0)),
                  cache_head, cache_head,
                  pl.BlockSpec((MLA_KV_RANK, LANES), lambda bi, h: (0, h)),
                  pl.BlockSpec((MLA_KV_RANK, LANES), lambda bi, h: (0, h)),
                  pl.BlockSpec((8, LANES), lambda bi, h: (0, 0)),
                  pl.BlockSpec((1, LANES), lambda bi, h: (0, 0))],
        out_specs=[new, new],
        out_shape=[out, out],
        compiler_params=_params(("parallel", "parallel")),
        name="attn_sample",
    )(rel_table, qm, qd, kmn, vmn, kdn, vdn, latc, krc, kdc, vdc, wk, wv, lamv, sg)


def _post_kernel(x_ref, om_ref, od_ref, g_ref, mod_ref, wom_ref, wod_ref, wout_ref, n2_ref,
                 wrh_ref, wrl_ref, br_ref, x1_ref, h2_ref, gate_ref):
    d = x_ref.shape[2]
    om = jnp.concatenate([om_ref[0, hd] for hd in range(HEADS)], axis=1)
    od = jnp.concatenate([od_ref[0, hd] for hd in range(HEADS)], axis=1)
    g = g_ref[0].astype(F32)
    merged = g[:, :d] * _dot(om, wom_ref[...]) + g[:, d:] * _dot(od, wod_ref[...])
    m = mod_ref[0]
    x1 = x_ref[0] + m[2:3] * _dot(merged.astype(BF16), wout_ref[...])
    x1_ref[0] = x1
    h2 = _rms(x1, n2_ref[...]) * (1.0 + m[4:5]) + m[3:4]
    h2_ref[0] = h2.astype(BF16)

    hi, lo = _split_bf16(h2)
    wrh = wrh_ref[...]
    logits = _dot(hi, wrh) + (_dot(hi, wrl_ref[...]) + _dot(lo, wrh)) + br_ref[...]
    lane = lax.broadcasted_iota(jnp.int32, logits.shape, 1).astype(F32)
    work = logits
    tops, picks = [], []
    for _ in range(TOP_K):
        top = jnp.max(work, axis=1, keepdims=True)
        first = jnp.min(jnp.where(work == top, lane, float(N_EXPERTS)), axis=1, keepdims=True)
        pick = lane == first
        tops.append(top)
        picks.append(pick)
        work = jnp.where(pick, -jnp.inf, work)
    es = [jnp.exp(top - tops[0]) for top in tops]
    den = es[0]
    for e in es[1:]:
        den = den + e
    gate = jnp.zeros(logits.shape, F32)
    for e, pick in zip(es, picks):
        gate = jnp.where(pick, e / den, gate)
    gate_ref[0] = gate


def _post(x, om, od, g, mod, tt, w):
    b, s, d = x.shape
    tok = lambda width: pl.BlockSpec((1, tt, width), lambda bi, i: (bi, i, 0))
    head = pl.BlockSpec((1, HEADS, tt, LANES), lambda bi, i: (bi, 0, i, 0))
    weights = [w["wom"], w["wod"], w["wout"], w["n2"], w["wrh"], w["wrl"], w["br"]]
    return pl.pallas_call(
        _post_kernel,
        grid=(b, s // tt),
        in_specs=[tok(d), head, head, tok(2 * d), pl.BlockSpec((1, 6, d), lambda bi, i: (bi, 0, 0))]
                 + [_const_spec(a.shape) for a in weights],
        out_specs=[tok(d), tok(d), tok(N_EXPERTS)],
        out_shape=[jax.ShapeDtypeStruct((b, s, d), F32),
                   jax.ShapeDtypeStruct((b, s, d), BF16),
                   jax.ShapeDtypeStruct((b, s, N_EXPERTS), F32)],
        compiler_params=_params(("parallel", "parallel")),
        name="post",
    )(x, om, od, g, mod, *weights)


def _moe_kernel(h_ref, gate_ref, win_ref, bin_ref, wdn_ref, bdn_ref, f_ref, acc_ref, *, f_chunk):
    e = pl.program_id(1)
    gate = gate_ref[...]
    d_exp = wdn_ref.shape[1]

    @pl.when(e == 0)
    def _():
        hi, lo = _split_bf16(gate)
        bdn = bdn_ref[...]
        acc_ref[...] = _dot(hi, bdn) + _dot(lo, bdn)

    lane = lax.broadcasted_iota(jnp.int32, gate.shape, 1)
    ge = jnp.sum(jnp.where(lane == e, gate, 0.0), axis=1, keepdims=True)
    h = h_ref[...]
    for c0 in range(0, d_exp, f_chunk):
        c1 = c0 + f_chunk
        gl = _dot(h, win_ref[0, :, c0:c1]) + bin_ref[0, :, c0:c1]
        up = _dot(h, win_ref[0, :, d_exp + c0:d_exp + c1]) + bin_ref[0, :, d_exp + c0:d_exp + c1]
        gl = jnp.minimum(gl, SWIGLU_LIMIT)
        up = jnp.clip(up, -SWIGLU_LIMIT, SWIGLU_LIMIT)
        act = gl * jax.nn.sigmoid(SWIGLU_ALPHA * gl) * (up + 1.0) * ge
        acc_ref[...] += _dot(act.astype(BF16), wdn_ref[0, c0:c1, :])

    @pl.when(e == pl.num_programs(1) - 1)
    def _():
        f_ref[...] = acc_ref[...]


def _moe(h2, gate, w, tm):
    n, d = h2.shape
    n_exp, _, two_f = w["win"].shape
    d_exp = two_f // 2
    assert n % tm == 0
    return pl.pallas_call(
        functools.partial(_moe_kernel, f_chunk=min(512, d_exp)),
        grid=(n // tm, n_exp),
        in_specs=[pl.BlockSpec((tm, d), lambda i, e: (i, 0)),
                  pl.BlockSpec((tm, n_exp), lambda i, e: (i, 0)),
                  pl.BlockSpec((1, d, two_f), lambda i, e: (e, 0, 0)),
                  pl.BlockSpec((1, 1, two_f), lambda i, e: (e, 0, 0)),
                  pl.BlockSpec((1, d_exp, d), lambda i, e: (e, 0, 0)),
                  pl.BlockSpec((n_exp, d), lambda i, e: (0, 0))],
        out_specs=pl.BlockSpec((tm, d), lambda i, e: (i, 0)),
        out_shape=jax.ShapeDtypeStruct((n, d), F32),
        scratch_shapes=[pltpu.VMEM((tm, d), F32)],
        compiler_params=_params(("parallel", "arbitrary")),
        name="moe",
    )(h2, gate, w["win"], w["bin"], w["wdn"], w["bdn"])


def _final_kernel(x1_ref, f_ref, mod_ref, g_ref, y_ref, *, normalize):
    y = x1_ref[0] + mod_ref[0][5:6] * f_ref[0]
    y_ref[0] = _rms(y, g_ref[...]) if normalize else y


def _final(x1, f, mod, g, tt, normalize):
    b, s, d = x1.shape
    tok = pl.BlockSpec((1, tt, d), lambda bi, i: (bi, i, 0))
    return pl.pallas_call(
        functools.partial(_final_kernel, normalize=normalize),
        grid=(b, s // tt),
        in_specs=[tok, tok, pl.BlockSpec((1, 6, d), lambda bi, i: (bi, 0, 0)),
                  pl.BlockSpec((1, d), lambda bi, i: (0, 0))],
        out_specs=tok,
        out_shape=jax.ShapeDtypeStruct((b, s, d), F32),
        compiler_params=_params(("parallel", "parallel")),
        name="final",
    )(x1, f, mod, g)


def _layer_weights(l, w_in, q_norm_g, w_uq, kv_norm_g, w_ukv, norm1_g, subln_g, w_o_mla, w_o_diff, w_out,
                   norm2_g, w_router, b_router, w_e_in, b_e_in, w_e_down, b_e_down):
    d = w_in.shape[1]
    hw = HEADS * LANES
    sizes = (MLA_Q_RANK, MLA_KV_RANK, MLA_ROPE, hw, hw, hw, 2 * d)
    offs = np.concatenate([[0], np.cumsum(sizes)])
    pieces = [w_in[l][:, offs[i]:offs[i + 1]] for i in range(len(sizes))]
    w_cq, w_ckv, w_kr, w_dq, w_dk, w_dv, w_g = pieces
    half = MLA_ROPE // 2
    pad_lo = jnp.zeros((d, ROPE_LO), F32)
    pad_hi = jnp.zeros((d, LANES - ROPE_HI), F32)
    kr_lanes = jnp.concatenate([pad_lo, w_kr, pad_hi], axis=1)
    kr_swap = jnp.concatenate([pad_lo, w_kr[:, half:], w_kr[:, :half], pad_hi], axis=1)
    w1 = jnp.concatenate([w_cq, w_ckv, kr_lanes, kr_swap], axis=1)

    uq = w_uq[l].reshape(MLA_Q_RANK, HEADS, MLA_NOPE + MLA_ROPE)
    nope, rope = uq[..., :MLA_NOPE], uq[..., MLA_NOPE:]
    zq = lambda n: jnp.zeros((MLA_Q_RANK, HEADS, n), F32)
    wq = jnp.concatenate([nope, rope, zq(LANES - ROPE_HI)], axis=-1).reshape(MLA_Q_RANK, hw)
    wqs = jnp.concatenate([zq(ROPE_LO), rope[..., half:], rope[..., :half], zq(LANES - ROPE_HI)],
                          axis=-1).reshape(MLA_Q_RANK, hw)
    ukv = w_ukv[l].reshape(MLA_KV_RANK, HEADS, MLA_NOPE + MLA_V)
    zk = lambda n: jnp.zeros((MLA_KV_RANK, HEADS, n), F32)
    wk = jnp.concatenate([ukv[..., :MLA_NOPE], zk(LANES - MLA_NOPE)], axis=-1).reshape(MLA_KV_RANK, hw)
    wv = jnp.concatenate([ukv[..., MLA_NOPE:], zk(LANES - MLA_V)], axis=-1).reshape(MLA_KV_RANK, hw)
    wom = jnp.concatenate([w_o_mla[l].reshape(HEADS, MLA_V, d), jnp.zeros((HEADS, LANES - MLA_V, d), F32)],
                          axis=1).reshape(hw, d)
    wrh, wrl = _split_bf16(w_router[l])
    bf = lambda a: a.astype(BF16)
    row = lambda a: a.reshape(1, -1)
    return dict(
        n1=row(norm1_g[l]), qg=row(q_norm_g[l]), kvg=row(kv_norm_g[l]),
        w1=bf(w1), wq=bf(wq), wqs=bf(wqs), wk=bf(wk), wv=bf(wv),
        wdq=bf(w_dq), wdk=bf(w_dk), wdv=bf(w_dv), wg=bf(w_g),
        sg=row(subln_g[l]), wom=bf(wom), wod=bf(w_o_diff[l]), wout=bf(w_out[l]),
        n2=row(norm2_g[l]), wrh=wrh, wrl=wrl, br=row(b_router[l]),
        win=bf(w_e_in[l]), bin=b_e_in[l][:, None, :], wdn=bf(w_e_down[l]), bdn=bf(b_e_down[l]),
    )


def _tile(n, target):
    t = min(n, target)
    while n % t:
        t //= 2
    return t


def kernel(x_prompt, x_sample, cache_mla_latent, cache_mla_krope, cache_diff_k, cache_diff_v,
           c_prompt, c_sample, rel_table, final_norm_g, w_ada, b_ada, norm1_g, w_in, q_norm_g, w_uq,
           kv_norm_g, w_ukv, lam_q1, lam_k1, lam_q2, lam_k2, subln_g, w_o_mla, w_o_diff, w_out,
           norm2_g, w_router, b_router, w_e_in, b_e_in, w_e_down, b_e_down):
    depth = w_ada.shape[0]
    bp, sp, d = x_prompt.shape
    bs, ss, _ = x_sample.shape
    past = cache_mla_latent.shape[2]
    assert HEADS * DIFF_VD == HEADS * LANES and past % ss == 0 and ss % SUBLANES_BF16 == 0

    n_pos = max(sp, past + ss)
    rope_rows = _tile(n_pos, 2048)
    cos_tab, sin_tab = _rope_tables(-(-n_pos // rope_rows) * rope_rows, rope_rows)

    pad_rows = -(bp + bs) % SUBLANES_BF16
    c_all = jnp.concatenate([c_prompt, c_sample, jnp.zeros((pad_rows, d), F32)], axis=0)

    g_fin = final_norm_g.reshape(1, d)
    hp, hs = x_prompt, x_sample
    rows_p, rows_s = [], []
    for l in range(depth):
        lam_init = 0.8 - 0.6 * math.exp(-0.3 * l)
        w = _layer_weights(l, w_in, q_norm_g, w_uq, kv_norm_g, w_ukv, norm1_g, subln_g, w_o_mla, w_o_diff,
                           w_out, norm2_g, w_router, b_router, w_e_in, b_e_in, w_e_down, b_e_down)
        lam_vecs = jnp.stack([lam_q1[l], lam_k1[l], lam_q2[l], lam_k2[l]])
        mod, lamv = _ada(c_all, w_ada[l].astype(BF16), b_ada[l].reshape(1, -1), lam_vecs, lam_init)
        mod_p = mod[:bp].reshape(bp, 6, d)
        mod_s = mod[bp:bp + bs].reshape(bs, 6, d)
        post_scale = 1.0 - lam_init

        tt = _tile(sp, 256)
        lat, kr, dkf, dvf, qm, km, vm, qd, kd, vd, g = _proj(hp, mod_p, cos_tab, sin_tab, 0, tt, w)
        om, od = _attn_prompt(rel_table, qm, qd, km, vm, kd, vd, lamv, w["sg"], _tile(sp, 256), _tile(sp, 256), post_scale)
        x1, h2, gate = _post(hp, om, od, g, mod_p, tt, w)
        f = _moe(h2.reshape(bp * sp, d), gate.reshape(bp * sp, N_EXPERTS), w, _tile(bp * sp, 1024))
        hp = (x1, f.reshape(bp, sp, d), mod_p)
        rows_p.append((lat, kr, dkf.reshape(bp, sp, HEADS, 2 * DIFF_HD), dvf.reshape(bp, sp, HEADS, DIFF_VD)))

        lat, kr, dkf, dvf, qm, km, vm, qd, kd, vd, g = _proj(hs, mod_s, cos_tab, sin_tab, past, ss, w)
        om, od = _attn_sample(rel_table, qm, qd, km, vm, kd, vd,
                              cache_mla_latent[l], cache_mla_krope[l],
                              cache_diff_k[l].reshape(bs, past, HEADS * 2 * DIFF_HD),
                              cache_diff_v[l].reshape(bs, past, HEADS * DIFF_VD),
                              w["wk"], w["wv"], lamv, w["sg"], post_scale)
        x1, h2, gate = _post(hs, om, od, g, mod_s, ss, w)
        f = _moe(h2.reshape(bs * ss, d), gate.reshape(bs * ss, N_EXPERTS), w, bs * ss)
        hs = (x1, f.reshape(bs, ss, d), mod_s)
        rows_s.append((lat, kr, dkf.reshape(bs, ss, HEADS, 2 * DIFF_HD), dvf.reshape(bs, ss, HEADS, DIFF_VD)))

        if l + 1 < depth:
            hp = _final(*hp, g_fin, _tile(sp, 1024), normalize=False)
            hs = _final(*hs, g_fin, ss, normalize=False)

    y_prompt = _final(*hp, g_fin, _tile(sp, 1024), normalize=True)
    y_sample = _final(*hs, g_fin, ss, normalize=True)
    stack = lambda rows, i: jnp.stack([r[i] for r in rows])
    return (y_prompt, y_sample,
            stack(rows_p, 0), stack(rows_p, 1), stack(rows_p, 2), stack(rows_p, 3),
            stack(rows_s, 0), stack(rows_s, 1), stack(rows_s, 2), stack(rows_s, 3))
```
